```python
import jax, jax.numpy as jnp
from jax import lax
import numpy as np

D_MODEL = 2048
BATCH = 8
SEQ = 2048
DEPTH = 1

MIX_WIDTH = D_MODEL
POOL_WIDTH = MIX_WIDTH // 2
N_POOL_GROUPS = 4
POOL_GROUP_DIM = POOL_WIDTH // N_POOL_GROUPS
POOL_WINDOWS = (2, 4, 8, 16)
RET_WIDTH = MIX_WIDTH - POOL_WIDTH
RET_HEADS = 8
RET_HEAD_DIM = RET_WIDTH // RET_HEADS
RET_CHUNK = 128
ROPE_BASE = 10000.0
IN_COLS = POOL_WIDTH + 4 * RET_WIDTH
D_FF = 5504
CONV_WIDTH = 3
PLE_DIM = 256
NORM_EPS = 1e-6
GN_EPS = 1e-5

kernel_name = "hybrid_pool_retention_convffn_ple"


def rmsnorm(x, w):
    xf = x.astype(jnp.float32)
    y = xf * lax.rsqrt(jnp.mean(xf * xf, axis=-1, keepdims=True) + NORM_EPS)
    return (y * w.astype(jnp.float32)).astype(x.dtype)


def pool_mixer(u, pool_w, pool_scale):
    b, s, _ = u.shape
    ug = u.reshape(b, s, N_POOL_GROUPS, POOL_GROUP_DIM)
    cs = jnp.cumsum(ug.astype(jnp.float32), axis=1)
    cs = jnp.pad(cs, ((0, 0), (1, 0), (0, 0), (0, 0)))
    t = jnp.arange(s)
    means = []
    for g, w in enumerate(POOL_WINDOWS):
        lo = jnp.maximum(t + 1 - w, 0)
        cnt = (t + 1 - lo).astype(jnp.float32)
        win = cs[:, 1:, g] - cs[:, lo, g]
        means.append(win / cnt[None, :, None])
    mean = jnp.stack(means, axis=2).astype(u.dtype)
    y = jnp.einsum('bsgc,gcd->bsgd', mean - ug, pool_w)
    return y.reshape(b, s, POOL_WIDTH) * pool_scale


def rotary(x, cos, sin):
    x1, x2 = jnp.split(x, 2, axis=-1)
    c = cos[None, :, None, :]
    s_ = sin[None, :, None, :]
    return jnp.concatenate([x1 * c - x2 * s_, x2 * c + x1 * s_], axis=-1)


def retention_chunkwise(q, k, v):
    b, s, h, dk = q.shape
    dv = v.shape[-1]
    n = s // RET_CHUNK
    C = RET_CHUNK
    qc = q.reshape(b, n, C, h, dk)
    kc = k.reshape(b, n, C, h, dk)
    vc = v.reshape(b, n, C, h, dv)
    gamma = 1.0 - jnp.exp2(-5.0 - jnp.arange(h, dtype=jnp.float32))
    log_g = jnp.log(gamma)
    idx = jnp.arange(C, dtype=jnp.float32)
    diff = idx[:, None] - idx[None, :]
    decay = jnp.where(diff[None] >= 0,
                      jnp.exp(jnp.maximum(diff, 0.0)[None] * log_g[:, None, None]),
                      0.0)
    scores = jnp.einsum('bnihd,bnjhd->bnhij', qc, kc) * decay
    intra = jnp.einsum('bnhij,bnjhe->bnihe', scores, vc)
    zeta = jnp.exp((C - 1.0 - idx)[None, :] * log_g[:, None])
    kv = jnp.einsum('bnjhd,hj,bnjhe->nbhde', kc, zeta, vc)
    g_chunk = jnp.exp(C * log_g)[:, None, None]

    def step(state, kv_i):
        return g_chunk * state + kv_i, state

    init = jnp.zeros((b, h, dk, dv), dtype=kv.dtype)
    _, r_prev = lax.scan(step, init, kv)
    xi = jnp.exp((idx + 1.0)[None, :] * log_g[:, None])
    cross = jnp.einsum('bnihd,nbhde,hi->bnihe', qc, r_prev, xi)
    return (intra + cross).reshape(b, s, h, dv).astype(q.dtype)


def head_groupnorm(y, w):
    yf = y.astype(jnp.float32)
    mu = jnp.mean(yf, axis=-1, keepdims=True)
    var = jnp.mean(jnp.square(yf - mu), axis=-1, keepdims=True)
    return ((yf - mu) * lax.rsqrt(var + GN_EPS) * w.astype(jnp.float32)).astype(y.dtype)


def causal_dwconv(x, w, bias):
    s = x.shape[1]
    xp = jnp.pad(x, ((0, 0), (CONV_WIDTH - 1, 0), (0, 0)))
    out = bias
    for j in range(CONV_WIDTH):
        out = out + xp[:, j:j + s] * w[j]
    return out


def setup_inputs(seed: int = 0) -> dict:
    key = jax.random.key(seed)
    ks = jax.random.split(key, 20)
    f32 = jnp.float32
    nrm = lambda k, shape, scale: jax.random.normal(k, shape, f32) * scale
    return {
        "x": nrm(ks[0], (BATCH, SEQ, D_MODEL), 1.0),
        "p": nrm(ks[1], (DEPTH, BATCH, SEQ, PLE_DIM), 1.0),
        "norm1_w": 1.0 + nrm(ks[2], (DEPTH, D_MODEL), 0.05),
        "w_in": nrm(ks[3], (DEPTH, D_MODEL, IN_COLS), D_MODEL ** -0.5),
        "pool_w": nrm(ks[4], (DEPTH, N_POOL_GROUPS, POOL_GROUP_DIM, POOL_GROUP_DIM), POOL_GROUP_DIM ** -0.5),
        "pool_scale": 1.0 + nrm(ks[5], (DEPTH, POOL_WIDTH), 0.1),
        "ret_gn_w": 1.0 + nrm(ks[6], (DEPTH, RET_WIDTH), 0.05),
        "w_out": nrm(ks[7], (DEPTH, MIX_WIDTH, D_MODEL), MIX_WIDTH ** -0.5),
        "norm2_w": 1.0 + nrm(ks[8], (DEPTH, D_MODEL), 0.05),
        "w_up": nrm(ks[9], (DEPTH, D_MODEL, 2 * D_FF), D_MODEL ** -0.5),
        "conv_w": nrm(ks[10], (DEPTH, CONV_WIDTH, 2 * D_FF), CONV_WIDTH ** -0.5),
        "conv_b": nrm(ks[11], (DEPTH, 2 * D_FF), 0.02),
        "w_down": nrm(ks[12], (DEPTH, D_FF, D_MODEL), D_FF ** -0.5),
        "norm3_w": 1.0 + nrm(ks[13], (DEPTH, D_MODEL), 0.05),
        "ple_gate_w": nrm(ks[14], (DEPTH, D_MODEL, D_MODEL), D_MODEL ** -0.5),
        "ple_proj_w": nrm(ks[15], (DEPTH, PLE_DIM, D_MODEL), PLE_DIM ** -0.5),
        "final_norm_w": 1.0 + nrm(ks[16], (D_MODEL,), 0.05),
    }


def reference(x, p, norm1_w, w_in, pool_w, pool_scale, ret_gn_w, w_out, norm2_w,
              w_up, conv_w, conv_b, w_down, norm3_w, ple_gate_w, ple_proj_w, final_norm_w):
    b, s, _ = x.shape
    pos = jnp.arange(s, dtype=jnp.float32)
    inv_freq = 1.0 / (ROPE_BASE ** (jnp.arange(0, RET_HEAD_DIM, 2, dtype=jnp.float32) / RET_HEAD_DIM))
    ang = pos[:, None] * inv_freq[None, :]
    cos = jnp.cos(ang).astype(x.dtype)
    sin = jnp.sin(ang).astype(x.dtype)
    k_scale = RET_HEAD_DIM ** -0.5
    splits = [POOL_WIDTH, POOL_WIDTH + RET_WIDTH, POOL_WIDTH + 2 * RET_WIDTH, POOL_WIDTH + 3 * RET_WIDTH]

    h = x
    for i in range(DEPTH):
        a = rmsnorm(h, norm1_w[i])
        z = a @ w_in[i]
        u, q, k, v, g = jnp.split(z, splits, axis=-1)
        pool_out = pool_mixer(u, pool_w[i], pool_scale[i])
        q = rotary(q.reshape(b, s, RET_HEADS, RET_HEAD_DIM), cos, sin)
        k = rotary(k.reshape(b, s, RET_HEADS, RET_HEAD_DIM), cos, sin) * k_scale
        v = v.reshape(b, s, RET_HEADS, RET_HEAD_DIM)
        r = retention_chunkwise(q, k, v)
        r = head_groupnorm(r, ret_gn_w[i].reshape(RET_HEADS, RET_HEAD_DIM)).reshape(b, s, RET_WIDTH)
        r = jax.nn.silu(g) * r
        h = h + jnp.concatenate([pool_out, r], axis=-1) @ w_out[i]
        c = rmsnorm(h, norm2_w[i])
        up = causal_dwconv(c @ w_up[i], conv_w[i], conv_b[i])
        gate, val = jnp.split(up, 2, axis=-1)
        h = h + (jax.nn.silu(gate) * val) @ w_down[i]
        e = rmsnorm(h, norm3_w[i])
        h = h + jax.nn.sigmoid(e @ ple_gate_w[i]) * (p[i] @ ple_proj_w[i])
    return rmsnorm(h, final_norm_w)
```

```python
import functools

import numpy as np
import jax
import jax.numpy as jnp
from jax import lax
from jax.experimental import pallas as pl
from jax.experimental.pallas import tpu as pltpu

D_MODEL = 2048
POOL_WIDTH = 1024
N_POOL_GROUPS = 4
POOL_GROUP_DIM = 256
POOL_WINDOWS = (2, 4, 8, 16)
RET_WIDTH = 1024
RET_HEADS = 8
RET_HEAD_DIM = 128
ROPE_BASE = 10000.0
IN_COLS = POOL_WIDTH + 4 * RET_WIDTH
D_FF = 5504
CONV_WIDTH = 3
NORM_EPS = 1e-6
GN_EPS = 1e-5

V7X_LANES = 128
V7X_SUBLANES = 8
V7X_VMEM_BYTES = 64 * 1024 * 1024

RET_CHUNK = 128
D_FF_PAD = 5632
FFN_TN = 512
FFN_TM = 512
IN_TM, IN_TN = 1024, 1024
OUT_TM, OUT_TN = 1024, 1024
PLE_TM = 512
PLE_TN = 512
ROW_CHUNK = 128

BF16 = jnp.bfloat16
F32 = jnp.float32


def _vmem_limit(block_bytes):
    return int(min(V7X_VMEM_BYTES - 6 * 1024 * 1024, block_bytes + 16 * 1024 * 1024))


def _params(block_bytes):
    return pltpu.CompilerParams(
        dimension_semantics=("arbitrary", "arbitrary"),
        vmem_limit_bytes=_vmem_limit(block_bytes),
    )


def _rms_rows(x, w):
    ms = jnp.mean(x * x, axis=-1, keepdims=True)
    return x * lax.rsqrt(ms + NORM_EPS) * w


def _in_proj_kernel(x_ref, nw_ref, w_ref, z_ref, a_scr):
    @pl.when(pl.program_id(1) == 0)
    def _():
        def body(r, carry):
            rows = pl.ds(pl.multiple_of(r * ROW_CHUNK, ROW_CHUNK), ROW_CHUNK)
            a_scr[rows, :] = _rms_rows(x_ref[rows, :], nw_ref[...]).astype(BF16)
            return carry

        lax.fori_loop(0, IN_TM // ROW_CHUNK, body, 0)

    z_ref[...] = jnp.dot(a_scr[...], w_ref[...], preferred_element_type=F32).astype(BF16)


def _in_proj(x2, norm1_w, w_in_b):
    m = x2.shape[0]
    blk = 2 * (IN_TM * D_MODEL * 4 + D_MODEL * IN_TN * 2 + IN_TM * IN_TN * 2) + IN_TM * D_MODEL * 2
    return pl.pallas_call(
        _in_proj_kernel,
        grid=(m // IN_TM, IN_COLS // IN_TN),
        in_specs=[
            pl.BlockSpec((IN_TM, D_MODEL), lambda i, j: (i, 0)),
            pl.BlockSpec((1, D_MODEL), lambda i, j: (0, 0)),
            pl.BlockSpec((D_MODEL, IN_TN), lambda i, j: (0, j)),
        ],
        out_specs=pl.BlockSpec((IN_TM, IN_TN), lambda i, j: (i, j)),
        out_shape=jax.ShapeDtypeStruct((m, IN_COLS), BF16),
        scratch_shapes=[pltpu.VMEM((IN_TM, D_MODEL), BF16)],
        compiler_params=_params(blk),
        name="in_proj",
    )(x2, norm1_w, w_in_b)


def _mixer_tables(seq):
    c = RET_CHUNK
    half = RET_HEAD_DIM // 2
    pos = np.arange(seq, dtype=np.float64)
    inv_freq = 1.0 / (ROPE_BASE ** (np.arange(0, RET_HEAD_DIM, 2, dtype=np.float64) / RET_HEAD_DIM))
    ang = pos[:, None] * inv_freq[None, :]
    cos, sin = np.cos(ang), np.sin(ang)
    k_scale = RET_HEAD_DIM ** -0.5
    cc = np.concatenate([cos, cos], axis=1)
    ss = np.concatenate([-sin, sin], axis=1)
    rope = np.concatenate([cc, ss, cc * k_scale, ss * k_scale], axis=1)

    gamma = 1.0 - np.exp2(-5.0 - np.arange(RET_HEADS, dtype=np.float64))
    idx = np.arange(c, dtype=np.float64)
    diff = idx[:, None] - idx[None, :]
    decay = np.where(diff[None] >= 0, gamma[:, None, None] ** np.maximum(diff, 0.0)[None], 0.0)
    xi = gamma[None, :] ** (idx[:, None] + 1.0)
    zeta = gamma[None, :] ** (c - 1.0 - idx[:, None])
    xi_t = np.repeat(xi, RET_HEAD_DIM, axis=1)
    zeta_t = np.repeat(zeta, RET_HEAD_DIM, axis=1)
    g_chunk = tuple(float(g) for g in gamma ** c)

    band = np.zeros((N_POOL_GROUPS, c, 2 * c), dtype=np.float32)
    for g, w in enumerate(POOL_WINDOWS):
        for m_ in range(w):
            band[g, np.arange(c), c + np.arange(c) - m_] = 1.0
    return (jnp.asarray(rope, F32), jnp.asarray(decay, F32), jnp.asarray(xi_t, F32),
            jnp.asarray(zeta_t, F32), jnp.asarray(band, BF16), g_chunk)


def _mixer_kernel(g_chunk, z_ref, rope_ref, decay_ref, xi_ref, zeta_ref, band_ref, pw_ref, ps_ref,
                  gn_ref, o_ref, ucat_scr, state_scr):
    c = RET_CHUNK
    hd = RET_HEAD_DIM
    ci = pl.program_id(1)

    @pl.when(ci == 0)
    def _():
        ucat_scr[...] = jnp.zeros_like(ucat_scr)
        state_scr[...] = jnp.zeros_like(state_scr)

    ucat_scr[c:, :] = z_ref[:, :POOL_WIDTH]
    t1 = ci * c + lax.broadcasted_iota(jnp.int32, (c, POOL_GROUP_DIM), 0) + 1
    for g, w in enumerate(POOL_WINDOWS):
        cols = slice(g * POOL_GROUP_DIM, (g + 1) * POOL_GROUP_DIM)
        wsum = jnp.dot(band_ref[g], ucat_scr[:, cols], preferred_element_type=F32)
        cnt = jnp.minimum(t1, w).astype(F32)
        d = wsum / cnt - ucat_scr[c:, cols].astype(F32)
        y = jnp.dot(d.astype(BF16), pw_ref[g], preferred_element_type=F32) * ps_ref[:, cols]
        o_ref[:, cols] = y.astype(BF16)
    ucat_scr[:c, :] = ucat_scr[c:, :]

    qcc, qss = rope_ref[:, 0:hd], rope_ref[:, hd:2 * hd]
    kcc, kss = rope_ref[:, 2 * hd:3 * hd], rope_ref[:, 3 * hd:4 * hd]
    for h in range(RET_HEADS):
        hc = slice(h * hd, (h + 1) * hd)
        q = z_ref[:, POOL_WIDTH + h * hd:POOL_WIDTH + (h + 1) * hd].astype(F32)
        k = z_ref[:, POOL_WIDTH + RET_WIDTH + h * hd:POOL_WIDTH + RET_WIDTH + (h + 1) * hd].astype(F32)
        v_b = z_ref[:, POOL_WIDTH + 2 * RET_WIDTH + h * hd:POOL_WIDTH + 2 * RET_WIDTH + (h + 1) * hd]
        gt = z_ref[:, POOL_WIDTH + 3 * RET_WIDTH + h * hd:POOL_WIDTH + 3 * RET_WIDTH + (h + 1) * hd].astype(F32)
        q_r = q * qcc + pltpu.roll(q, hd // 2, axis=1) * qss
        k_r = k * kcc + pltpu.roll(k, hd // 2, axis=1) * kss
        q_b = q_r.astype(BF16)
        k_b = k_r.astype(BF16)
        k_z = (k_r * zeta_ref[:, hc]).astype(BF16)
        sc = lax.dot_general(q_b, k_b, (((1,), (1,)), ((), ())), preferred_element_type=F32)
        p_b = (sc * decay_ref[h]).astype(BF16)
        intra = jnp.dot(p_b, v_b, preferred_element_type=F32)
        st = state_scr[h]
        cross = jnp.dot(q_b, st.astype(BF16), preferred_element_type=F32) * xi_ref[:, hc]
        r = intra + cross
        kv = lax.dot_general(k_z, v_b, (((0,), (0,)), ((), ())), preferred_element_type=F32)
        state_scr[h] = g_chunk[h] * st + kv
        mu = jnp.mean(r, axis=-1, keepdims=True)
        xc = r - mu
        var = jnp.mean(xc * xc, axis=-1, keepdims=True)
        rn = xc * lax.rsqrt(var + GN_EPS) * gn_ref[:, hc]
        silu = gt / (1.0 + jnp.exp(-gt))
        o_ref[:, POOL_WIDTH + h * hd:POOL_WIDTH + (h + 1) * hd] = (silu * rn).astype(BF16)


def _mixer(z, pool_w_b, pool_scale, ret_gn_w, batch, seq):
    c = RET_CHUNK
    n_chunks = seq // c
    rope, decay, xi_t, zeta_t, band, g_chunk = _mixer_tables(seq)
    blk = 2 * (c * IN_COLS * 2 + c * 4 * RET_HEAD_DIM * 4 + RET_HEADS * c * c * 4 + 2 * c * RET_WIDTH * 4
               + N_POOL_GROUPS * c * 2 * c * 2 + N_POOL_GROUPS * POOL_GROUP_DIM * POOL_GROUP_DIM * 2
               + c * D_MODEL * 2) + 2 * c * POOL_WIDTH * 2 + RET_HEADS * RET_HEAD_DIM * RET_HEAD_DIM * 4
    const2 = lambda b, ci: (0, 0)
    const3 = lambda b, ci: (0, 0, 0)
    return pl.pallas_call(
        functools.partial(_mixer_kernel, g_chunk),
        grid=(batch, n_chunks),
        in_specs=[
            pl.BlockSpec((c, IN_COLS), lambda b, ci: (b * n_chunks + ci, 0)),
            pl.BlockSpec((c, 4 * RET_HEAD_DIM), lambda b, ci: (ci, 0)),
            pl.BlockSpec((RET_HEADS, c, c), const3),
            pl.BlockSpec((c, RET_WIDTH), const2),
            pl.BlockSpec((c, RET_WIDTH), const2),
            pl.BlockSpec((N_POOL_GROUPS, c, 2 * c), const3),
            pl.BlockSpec((N_POOL_GROUPS, POOL_GROUP_DIM, POOL_GROUP_DIM), const3),
            pl.BlockSpec((1, POOL_WIDTH), const2),
            pl.BlockSpec((1, RET_WIDTH), const2),
        ],
        out_specs=pl.BlockSpec((c, D_MODEL), lambda b, ci: (b * n_chunks + ci, 0)),
        out_shape=jax.ShapeDtypeStruct((batch * seq, D_MODEL), BF16),
        scratch_shapes=[
            pltpu.VMEM((2 * c, POOL_WIDTH), BF16),
            pltpu.VMEM((RET_HEADS, RET_HEAD_DIM, RET_HEAD_DIM), F32),
        ],
        compiler_params=_params(blk),
        name="mixer",
    )(z, rope, decay, xi_t, zeta_t, band, pool_w_b, pool_scale, ret_gn_w)


def _out_proj_kernel(mix_ref, w_ref, x_ref, o_ref):
    o_ref[...] = x_ref[...] + jnp.dot(mix_ref[...], w_ref[...], preferred_element_type=F32)


def _out_proj(mix, w_out_b, x2):
    m = x2.shape[0]
    blk = 2 * (OUT_TM * D_MODEL * 2 + D_MODEL * OUT_TN * 2 + 2 * OUT_TM * OUT_TN * 4)
    return pl.pallas_call(
        _out_proj_kernel,
        grid=(m // OUT_TM, D_MODEL // OUT_TN),
        in_specs=[
            pl.BlockSpec((OUT_TM, D_MODEL), lambda i, j: (i, 0)),
            pl.BlockSpec((D_MODEL, OUT_TN), lambda i, j: (0, j)),
            pl.BlockSpec((OUT_TM, OUT_TN), lambda i, j: (i, j)),
        ],
        out_specs=pl.BlockSpec((OUT_TM, OUT_TN), lambda i, j: (i, j)),
        out_shape=jax.ShapeDtypeStruct((m, D_MODEL), F32),
        compiler_params=_params(blk),
        name="out_proj",
    )(mix, w_out_b, x2)


def _dwconv_tile(up, halo, cw_ref, cb_ref):
    tm = up.shape[0]
    row = lax.broadcasted_iota(jnp.int32, (V7X_SUBLANES, up.shape[1]), 0)
    out = cb_ref[...] + cw_ref[CONV_WIDTH - 1:CONV_WIDTH, :] * up
    for s in range(1, CONV_WIDTH):
        rolled = pltpu.roll(up, s, axis=0)
        head = jnp.where(row < s, pltpu.roll(halo, s, axis=0), rolled[:V7X_SUBLANES])
        shifted = jnp.concatenate([head, rolled[V7X_SUBLANES:]], axis=0) if tm > V7X_SUBLANES else head
        out = out + cw_ref[CONV_WIDTH - 1 - s:CONV_WIDTH - s, :] * shifted
    return out


def _ffn_kernel(tiles_per_seq, h_ref, nw_ref, wg_ref, wv_ref, cwg_ref, cwv_ref, cbg_ref, cbv_ref, wd_ref,
                o_ref, c_scr, hg_scr, hv_scr):
    i = pl.program_id(0)
    j = pl.program_id(1)

    @pl.when(jnp.logical_and(i == 0, j == 0))
    def _():
        hg_scr[...] = jnp.zeros_like(hg_scr)
        hv_scr[...] = jnp.zeros_like(hv_scr)

    @pl.when(j == 0)
    def _():
        def body(r, carry):
            rows = pl.ds(pl.multiple_of(r * ROW_CHUNK, ROW_CHUNK), ROW_CHUNK)
            hrow = h_ref[rows, :]
            c_scr[rows, :] = _rms_rows(hrow, nw_ref[...]).astype(BF16)
            o_ref[rows, :] = hrow
            return carry

        lax.fori_loop(0, FFN_TM // ROW_CHUNK, body, 0)

    c_b = c_scr[...]
    up_g = jnp.dot(c_b, wg_ref[...], preferred_element_type=F32)
    up_v = jnp.dot(c_b, wv_ref[...], preferred_element_type=F32)
    gate = _dwconv_tile(up_g, hg_scr[j], cwg_ref, cbg_ref)
    val = _dwconv_tile(up_v, hv_scr[j], cwv_ref, cbv_ref)
    keep = (i + 1) % tiles_per_seq != 0
    hg_scr[j] = jnp.where(keep, up_g[FFN_TM - V7X_SUBLANES:], 0.0)
    hv_scr[j] = jnp.where(keep, up_v[FFN_TM - V7X_SUBLANES:], 0.0)
    act = (gate / (1.0 + jnp.exp(-gate)) * val).astype(BF16)
    o_ref[...] += jnp.dot(act, wd_ref[...], preferred_element_type=F32)


def _ffn(h1, norm2_w, w_up_p, conv_w_p, conv_b_p, w_down_p, seq):
    m = h1.shape[0]
    nj = D_FF_PAD // FFN_TN
    blk = (2 * (2 * FFN_TM * D_MODEL * 4 + 2 * D_MODEL * FFN_TN * 2 + FFN_TN * D_MODEL * 2)
           + FFN_TM * D_MODEL * 2 + 2 * nj * V7X_SUBLANES * FFN_TN * 4 + 6 * FFN_TM * FFN_TN * 4)
    return pl.pallas_call(
        functools.partial(_ffn_kernel, seq // FFN_TM),
        grid=(m // FFN_TM, nj),
        in_specs=[
            pl.BlockSpec((FFN_TM, D_MODEL), lambda i, j: (i, 0)),
            pl.BlockSpec((1, D_MODEL), lambda i, j: (0, 0)),
            pl.BlockSpec((D_MODEL, FFN_TN), lambda i, j: (0, j)),
            pl.BlockSpec((D_MODEL, FFN_TN), lambda i, j: (0, nj + j)),
            pl.BlockSpec((CONV_WIDTH, FFN_TN), lambda i, j: (0, j)),
            pl.BlockSpec((CONV_WIDTH, FFN_TN), lambda i, j: (0, nj + j)),
            pl.BlockSpec((1, FFN_TN), lambda i, j: (0, j)),
            pl.BlockSpec((1, FFN_TN), lambda i, j: (0, nj + j)),
            pl.BlockSpec((FFN_TN, D_MODEL), lambda i, j: (j, 0)),
        ],
        out_specs=pl.BlockSpec((FFN_TM, D_MODEL), lambda i, j: (i, 0)),
        out_shape=jax.ShapeDtypeStruct((m, D_MODEL), F32),
        scratch_shapes=[
            pltpu.VMEM((FFN_TM, D_MODEL), BF16),
            pltpu.VMEM((nj, V7X_SUBLANES, FFN_TN), F32),
            pltpu.VMEM((nj, V7X_SUBLANES, FFN_TN), F32),
        ],
        compiler_params=_params(blk),
        name="ffn",
    )(h1, norm2_w, w_up_p, w_up_p, conv_w_p, conv_w_p, conv_b_p, conv_b_p, w_down_p)


def _ple_kernel(h_ref, p_ref, n3_ref, wg_ref, wp_ref, nf_ref, o_ref, e_scr, pb_scr):
    def prep(r, carry):
        rows = pl.ds(pl.multiple_of(r * ROW_CHUNK, ROW_CHUNK), ROW_CHUNK)
        e_scr[rows, :] = _rms_rows(h_ref[rows, :], n3_ref[...]).astype(BF16)
        pb_scr[rows, :] = p_ref[rows, :].astype(BF16)
        return carry

    lax.fori_loop(0, PLE_TM // ROW_CHUNK, prep, 0)

    e_b = e_scr[...]
    p_b = pb_scr[...]
    for n in range(D_MODEL // PLE_TN):
        cols = slice(n * PLE_TN, (n + 1) * PLE_TN)
        gl = jnp.dot(e_b, wg_ref[:, cols], preferred_element_type=F32)
        pp = jnp.dot(p_b, wp_ref[:, cols], preferred_element_type=F32)
        o_ref[:, cols] = h_ref[:, cols] + pp / (1.0 + jnp.exp(-gl))

    def fin(r, carry):
        rows = pl.ds(pl.multiple_of(r * ROW_CHUNK, ROW_CHUNK), ROW_CHUNK)
        o_ref[rows, :] = _rms_rows(o_ref[rows, :], nf_ref[...])
        return carry

    lax.fori_loop(0, PLE_TM // ROW_CHUNK, fin, 0)


def _ple(h2, p2, norm3_w, ple_gate_b, ple_proj_b, final_norm_w):
    m = h2.shape[0]
    ple_dim = p2.shape[1]
    blk = (2 * (2 * PLE_TM * D_MODEL * 4 + PLE_TM * ple_dim * 4 + D_MODEL * D_MODEL * 2 + ple_dim * D_MODEL * 2)
           + PLE_TM * D_MODEL * 2 + PLE_TM * ple_dim * 2 + 4 * PLE_TM * PLE_TN * 4)
    const = lambda i, j: (0, 0)
    return pl.pallas_call(
        _ple_kernel,
        grid=(m // PLE_TM, 1),
        in_specs=[
            pl.BlockSpec((PLE_TM, D_MODEL), lambda i, j: (i, 0)),
            pl.BlockSpec((PLE_TM, ple_dim), lambda i, j: (i, 0)),
            pl.BlockSpec((1, D_MODEL), const),
            pl.BlockSpec((D_MODEL, D_MODEL), const),
            pl.BlockSpec((ple_dim, D_MODEL), const),
            pl.BlockSpec((1, D_MODEL), const),
        ],
        out_specs=pl.BlockSpec((PLE_TM, D_MODEL), lambda i, j: (i, 0)),
        out_shape=jax.ShapeDtypeStruct((m, D_MODEL), F32),
        scratch_shapes=[pltpu.VMEM((PLE_TM, D_MODEL), BF16), pltpu.VMEM((PLE_TM, ple_dim), BF16)],
        compiler_params=_params(blk),
        name="ple",
    )(h2, p2, norm3_w, ple_gate_b, ple_proj_b, final_norm_w)


def _pad_ff(a, axis):
    pad = [(0, 0)] * a.ndim
    pad[axis] = (0, D_FF_PAD - D_FF)
    return jnp.pad(a, pad)


def kernel(x, p, norm1_w, w_in, pool_w, pool_scale, ret_gn_w, w_out, norm2_w, w_up, conv_w, conv_b, w_down,
           norm3_w, ple_gate_w, ple_proj_w, final_norm_w):
    batch, seq, d_model = x.shape
    depth = w_in.shape[0]
    assert d_model == D_MODEL and seq % RET_CHUNK == 0 and seq % FFN_TM == 0
    m = batch * seq
    assert m % IN_TM == 0 and m % OUT_TM == 0 and m % PLE_TM == 0
    h = x.reshape(m, D_MODEL)
    for l in range(depth):
        w_in_b = w_in[l].astype(BF16)
        pool_w_b = pool_w[l].astype(BF16)
        w_out_b = w_out[l].astype(BF16)
        w_up_p = jnp.concatenate([_pad_ff(w_up[l][:, :D_FF], 1), _pad_ff(w_up[l][:, D_FF:], 1)], axis=1).astype(BF16)
        conv_w_p = jnp.concatenate([_pad_ff(conv_w[l][:, :D_FF], 1), _pad_ff(conv_w[l][:, D_FF:], 1)], axis=1)
        conv_b_p = jnp.concatenate([_pad_ff(conv_b[l][:D_FF], 0), _pad_ff(conv_b[l][D_FF:], 0)])[None, :]
        w_down_p = _pad_ff(w_down[l], 0).astype(BF16)
        ple_gate_b = ple_gate_w[l].astype(BF16)
        ple_proj_b = ple_proj_w[l].astype(BF16)

        z = _in_proj(h, norm1_w[l][None, :], w_in_b)
        mix = _mixer(z, pool_w_b, pool_scale[l][None, :], ret_gn_w[l][None, :], batch, seq)
        h1 = _out_proj(mix, w_out_b, h)
        h2 = _ffn(h1, norm2_w[l][None, :], w_up_p, conv_w_p, conv_b_p, w_down_p, seq)
        is_last = l == depth - 1
        assert is_last, "multi-layer stacks need a ple variant without the final norm"
        h = _ple(h2, p[l].reshape(m, -1), norm3_w[l][None, :], ple_gate_b, ple_proj_b, final_norm_w[None, :])
    return h.reshape(batch, seq, D_MODEL)
```

```python
import functools

import numpy as np
import jax
import jax.numpy as jnp
from jax import lax
from jax.experimental import pallas as pl
from jax.experimental.pallas import tpu as pltpu

D_MODEL = 2048
POOL_WIDTH = 1024
N_POOL_GROUPS = 4
POOL_GROUP_DIM = 256
POOL_WINDOWS = (2, 4, 8, 16)
RET_WIDTH = 1024
RET_HEADS = 8
RET_HEAD_DIM = 128
ROPE_BASE = 10000.0
IN_COLS = POOL_WIDTH + 4 * RET_WIDTH
D_FF = 5504
CONV_WIDTH = 3
NORM_EPS = 1e-6
GN_EPS = 1e-5

V7X_LANES = 128
V7X_SUBLANES = 8
V7X_VMEM_BYTES = 64 * 1024 * 1024

RET_CHUNK = 128
D_FF_PAD = 5632
FFN_TN = 512
FFN_TM = 1024
SUB = 256
IN_TM = 512
OUT_TM = 512
PLE_TM = 512
PLE_TN = 512
ROW_CHUNK = 128

BF16 = jnp.bfloat16
F32 = jnp.float32


def _vmem_limit(block_bytes):
    return int(min(V7X_VMEM_BYTES - 6 * 1024 * 1024, block_bytes + 16 * 1024 * 1024))


def _params(block_bytes):
    return pltpu.CompilerParams(
        dimension_semantics=("arbitrary", "arbitrary"),
        vmem_limit_bytes=_vmem_limit(block_bytes),
    )


def _resident(shape):
    return pl.BlockSpec(shape, lambda i, j: (0,) * len(shape), pipeline_mode=pl.Buffered(1))


def _rms_rows(x, w):
    ms = jnp.mean(x * x, axis=-1, keepdims=True)
    return x * lax.rsqrt(ms + NORM_EPS) * w


def _in_proj_kernel(x_ref, nw_ref, w_ref, z_ref, a_scr):
    n_sub = IN_TM // SUB
    for s in range(n_sub):
        rows = slice(s * SUB, (s + 1) * SUB)
        a_scr[rows, :] = _rms_rows(x_ref[rows, :], nw_ref[...]).astype(BF16)
    for s in range(n_sub):
        rows = slice(s * SUB, (s + 1) * SUB)
        z_ref[rows, :] = jnp.dot(a_scr[rows, :], w_ref[...], preferred_element_type=F32).astype(BF16)


def _in_proj(x2, norm1_w, w_in_b):
    m = x2.shape[0]
    blk = (2 * (IN_TM * D_MODEL * 4 + IN_TM * IN_COLS * 2) + D_MODEL * IN_COLS * 2 + IN_TM * D_MODEL * 2)
    return pl.pallas_call(
        _in_proj_kernel,
        grid=(m // IN_TM, 1),
        in_specs=[
            pl.BlockSpec((IN_TM, D_MODEL), lambda i, j: (i, 0)),
            _resident((1, D_MODEL)),
            _resident((D_MODEL, IN_COLS)),
        ],
        out_specs=pl.BlockSpec((IN_TM, IN_COLS), lambda i, j: (i, 0)),
        out_shape=jax.ShapeDtypeStruct((m, IN_COLS), BF16),
        scratch_shapes=[pltpu.VMEM((IN_TM, D_MODEL), BF16)],
        compiler_params=_params(blk),
        name="in_proj",
    )(x2, norm1_w, w_in_b)


def _mixer_tables(seq):
    c = RET_CHUNK
    pos = np.arange(seq, dtype=np.float64)
    inv_freq = 1.0 / (ROPE_BASE ** (np.arange(0, RET_HEAD_DIM, 2, dtype=np.float64) / RET_HEAD_DIM))
    ang = pos[:, None] * inv_freq[None, :]
    cos, sin = np.cos(ang), np.sin(ang)
    k_scale = RET_HEAD_DIM ** -0.5
    cc = np.concatenate([cos, cos], axis=1)
    ss = np.concatenate([-sin, sin], axis=1)
    rope = np.concatenate([cc, ss, cc * k_scale, ss * k_scale], axis=1)

    gamma = 1.0 - np.exp2(-5.0 - np.arange(RET_HEADS, dtype=np.float64))
    idx = np.arange(c, dtype=np.float64)
    diff = idx[:, None] - idx[None, :]
    decay = np.where(diff[None] >= 0, gamma[:, None, None] ** np.maximum(diff, 0.0)[None], 0.0)
    xi = gamma[None, :] ** (idx[:, None] + 1.0)
    zeta = gamma[None, :] ** (c - 1.0 - idx[:, None])
    xi_t = np.repeat(xi, RET_HEAD_DIM, axis=1)
    zeta_t = np.repeat(zeta, RET_HEAD_DIM, axis=1)
    g_chunk = tuple(float(g) for g in gamma ** c)

    band = np.zeros((N_POOL_GROUPS, c, 2 * c), dtype=np.float32)
    for g, w in enumerate(POOL_WINDOWS):
        for m_ in range(w):
            band[g, np.arange(c), c + np.arange(c) - m_] = 1.0
    return (jnp.asarray(rope, F32), jnp.asarray(decay, F32), jnp.asarray(xi_t, F32),
            jnp.asarray(zeta_t, F32), jnp.asarray(band, BF16), g_chunk)


def _mixer_kernel(g_chunk, z_ref, rope_ref, decay_ref, xi_ref, zeta_ref, band_ref, pw_ref, ps_ref,
                  gn_ref, o_ref, ucat_scr, state_scr):
    c = RET_CHUNK
    hd = RET_HEAD_DIM
    ci = pl.program_id(1)

    @pl.when(ci == 0)
    def _():
        ucat_scr[...] = jnp.zeros_like(ucat_scr)
        state_scr[...] = jnp.zeros_like(state_scr)

    ucat_scr[c:, :] = z_ref[:, :POOL_WIDTH]
    t1 = ci * c + lax.broadcasted_iota(jnp.int32, (c, POOL_GROUP_DIM), 0) + 1
    for g, w in enumerate(POOL_WINDOWS):
        cols = slice(g * POOL_GROUP_DIM, (g + 1) * POOL_GROUP_DIM)
        wsum = jnp.dot(band_ref[g], ucat_scr[:, cols], preferred_element_type=F32)
        cnt = jnp.minimum(t1, w).astype(F32)
        d = wsum / cnt - ucat_scr[c:, cols].astype(F32)
        y = jnp.dot(d.astype(BF16), pw_ref[g], preferred_element_type=F32) * ps_ref[:, cols]
        o_ref[:, cols] = y.astype(BF16)
    ucat_scr[:c, :] = ucat_scr[c:, :]

    qcc, qss = rope_ref[:, 0:hd], rope_ref[:, hd:2 * hd]
    kcc, kss = rope_ref[:, 2 * hd:3 * hd], rope_ref[:, 3 * hd:4 * hd]
    for h in range(RET_HEADS):
        hc = slice(h * hd, (h + 1) * hd)
        q = z_ref[:, POOL_WIDTH + h * hd:POOL_WIDTH + (h + 1) * hd].astype(F32)
        k = z_ref[:, POOL_WIDTH + RET_WIDTH + h * hd:POOL_WIDTH + RET_WIDTH + (h + 1) * hd].astype(F32)
        v_b = z_ref[:, POOL_WIDTH + 2 * RET_WIDTH + h * hd:POOL_WIDTH + 2 * RET_WIDTH + (h + 1) * hd]
        gt = z_ref[:, POOL_WIDTH + 3 * RET_WIDTH + h * hd:POOL_WIDTH + 3 * RET_WIDTH + (h + 1) * hd].astype(F32)
        q_r = q * qcc + pltpu.roll(q, hd // 2, axis=1) * qss
        k_r = k * kcc + pltpu.roll(k, hd // 2, axis=1) * kss
        q_b = q_r.astype(BF16)
        k_b = k_r.astype(BF16)
        k_z = (k_r * zeta_ref[:, hc]).astype(BF16)
        sc = lax.dot_general(q_b, k_b, (((1,), (1,)), ((), ())), preferred_element_type=F32)
        p_b = (sc * decay_ref[h]).astype(BF16)
        intra = jnp.dot(p_b, v_b, preferred_element_type=F32)
        st = state_scr[h]
        cross = jnp.dot(q_b, st.astype(BF16), preferred_element_type=F32) * xi_ref[:, hc]
        r = intra + cross
        kv = lax.dot_general(k_z, v_b, (((0,), (0,)), ((), ())), preferred_element_type=F32)
        state_scr[h] = g_chunk[h] * st + kv
        mu = jnp.mean(r, axis=-1, keepdims=True)
        xc = r - mu
        var = jnp.mean(xc * xc, axis=-1, keepdims=True)
        rn = xc * lax.rsqrt(var + GN_EPS) * gn_ref[:, hc]
        silu = gt / (1.0 + jnp.exp(-gt))
        o_ref[:, POOL_WIDTH + h * hd:POOL_WIDTH + (h + 1) * hd] = (silu * rn).astype(BF16)


def _mixer(z, pool_w_b, pool_scale, ret_gn_w, batch, seq):
    c = RET_CHUNK
    n_chunks = seq // c
    rope, decay, xi_t, zeta_t, band, g_chunk = _mixer_tables(seq)
    blk = 2 * (c * IN_COLS * 2 + c * 4 * RET_HEAD_DIM * 4 + RET_HEADS * c * c * 4 + 2 * c * RET_WIDTH * 4
               + N_POOL_GROUPS * c * 2 * c * 2 + N_POOL_GROUPS * POOL_GROUP_DIM * POOL_GROUP_DIM * 2
               + c * D_MODEL * 2) + 2 * c * POOL_WIDTH * 2 + RET_HEADS * RET_HEAD_DIM * RET_HEAD_DIM * 4
    const2 = lambda b, ci: (0, 0)
    const3 = lambda b, ci: (0, 0, 0)
    return pl.pallas_call(
        functools.partial(_mixer_kernel, g_chunk),
        grid=(batch, n_chunks),
        in_specs=[
            pl.BlockSpec((c, IN_COLS), lambda b, ci: (b * n_chunks + ci, 0)),
            pl.BlockSpec((c, 4 * RET_HEAD_DIM), lambda b, ci: (ci, 0)),
            pl.BlockSpec((RET_HEADS, c, c), const3),
            pl.BlockSpec((c, RET_WIDTH), const2),
            pl.BlockSpec((c, RET_WIDTH), const2),
            pl.BlockSpec((N_POOL_GROUPS, c, 2 * c), const3),
            pl.BlockSpec((N_POOL_GROUPS, POOL_GROUP_DIM, POOL_GROUP_DIM), const3),
            pl.BlockSpec((1, POOL_WIDTH), const2),
            pl.BlockSpec((1, RET_WIDTH), const2),
        ],
        out_specs=pl.BlockSpec((c, D_MODEL), lambda b, ci: (b * n_chunks + ci, 0)),
        out_shape=jax.ShapeDtypeStruct((batch * seq, D_MODEL), BF16),
        scratch_shapes=[
            pltpu.VMEM((2 * c, POOL_WIDTH), BF16),
            pltpu.VMEM((RET_HEADS, RET_HEAD_DIM, RET_HEAD_DIM), F32),
        ],
        compiler_params=_params(blk),
        name="mixer",
    )(z, rope, decay, xi_t, zeta_t, band, pool_w_b, pool_scale, ret_gn_w)


def _out_proj_kernel(mix_ref, w_ref, x_ref, nw_ref, h_ref, c_ref):
    n_sub = OUT_TM // SUB
    acc = [jnp.dot(mix_ref[s * SUB:(s + 1) * SUB, :], w_ref[...], preferred_element_type=F32)
           for s in range(n_sub)]
    for s in range(n_sub):
        rows = slice(s * SUB, (s + 1) * SUB)
        h = x_ref[rows, :] + acc[s]
        h_ref[rows, :] = h
        c_ref[rows, :] = _rms_rows(h, nw_ref[...]).astype(BF16)


def _out_proj(mix, w_out_b, x2, norm2_w):
    m = x2.shape[0]
    blk = 2 * (OUT_TM * D_MODEL * 2 + 2 * OUT_TM * D_MODEL * 4 + OUT_TM * D_MODEL * 2) + D_MODEL * D_MODEL * 2
    row_spec = pl.BlockSpec((OUT_TM, D_MODEL), lambda i, j: (i, 0))
    return pl.pallas_call(
        _out_proj_kernel,
        grid=(m // OUT_TM, 1),
        in_specs=[row_spec, _resident((D_MODEL, D_MODEL)), row_spec, _resident((1, D_MODEL))],
        out_specs=[row_spec, row_spec],
        out_shape=[jax.ShapeDtypeStruct((m, D_MODEL), F32), jax.ShapeDtypeStruct((m, D_MODEL), BF16)],
        compiler_params=_params(blk),
        name="out_proj",
    )(mix, w_out_b, x2, norm2_w)


FFN_SLABS = FFN_TN // V7X_LANES


def _ffn_kernel(tiles_per_seq, c_ref, wg_ref, wv_ref, cwg_ref, cwv_ref, cbg_ref, cbv_ref, wd_ref,
                o_ref, hg_scr, hv_scr, ug_scr, uv_scr):
    i = pl.program_id(0)
    j = pl.program_id(1)
    hh = V7X_SUBLANES

    @pl.when(jnp.logical_and(i == 0, j == 0))
    def _():
        hg_scr[...] = jnp.zeros_like(hg_scr)
        hv_scr[...] = jnp.zeros_like(hv_scr)

    @pl.when(j == 0)
    def _():
        o_ref[...] = jnp.zeros_like(o_ref)

    ug_scr[:, 0:hh, :] = hg_scr[j]
    uv_scr[:, 0:hh, :] = hv_scr[j]

    def up_store(s):
        c_b = c_ref[s * SUB:(s + 1) * SUB, :]
        base = hh + s * SUB
        ug = jnp.dot(c_b, wg_ref[...], preferred_element_type=F32)
        uv = jnp.dot(c_b, wv_ref[...], preferred_element_type=F32)
        for ct in range(FFN_SLABS):
            lanes = slice(ct * V7X_LANES, (ct + 1) * V7X_LANES)
            ug_scr[ct, base:base + SUB, :] = ug[:, lanes]
            uv_scr[ct, base:base + SUB, :] = uv[:, lanes]

    def conv(u_scr, s, cw, cb):
        base = hh + s * SUB
        outs = []
        for ct in range(FFN_SLABS):
            lanes = slice(ct * V7X_LANES, (ct + 1) * V7X_LANES)
            out = cb[:, lanes] + cw[CONV_WIDTH - 1:CONV_WIDTH, lanes] * u_scr[ct, base:base + SUB, :]
            for k in range(1, CONV_WIDTH):
                out = out + cw[CONV_WIDTH - 1 - k:CONV_WIDTH - k, lanes] * u_scr[ct, base - k:base - k + SUB, :]
            outs.append(out)
        return jnp.concatenate(outs, axis=1)

    n_sub = FFN_TM // SUB
    up_store(0)
    for s in range(n_sub):
        if s + 1 < n_sub:
            up_store(s + 1)
        gate = conv(ug_scr, s, cwg_ref[...], cbg_ref[...])
        val = conv(uv_scr, s, cwv_ref[...], cbv_ref[...])
        act = (gate / (1.0 + jnp.exp(-gate)) * val).astype(BF16)
        o_ref[s * SUB:(s + 1) * SUB, :] += jnp.dot(act, wd_ref[...], preferred_element_type=F32)
    keep = (i + 1) % tiles_per_seq != 0
    hg_scr[j] = jnp.where(keep, ug_scr[:, FFN_TM:FFN_TM + hh, :], 0.0)
    hv_scr[j] = jnp.where(keep, uv_scr[:, FFN_TM:FFN_TM + hh, :], 0.0)


def _ffn(c, w_up_p, conv_w_p, conv_b_p, w_down_p, seq):
    m = c.shape[0]
    nj = D_FF_PAD // FFN_TN
    halo = (nj, FFN_SLABS, V7X_SUBLANES, V7X_LANES)
    slab = (FFN_SLABS, V7X_SUBLANES + FFN_TM, V7X_LANES)
    blk = (2 * (FFN_TM * D_MODEL * 2 + FFN_TM * D_MODEL * 4 + 2 * D_MODEL * FFN_TN * 2 + FFN_TN * D_MODEL * 2)
           + 2 * int(np.prod(halo)) * 4 + 2 * int(np.prod(slab)) * 4)
    return pl.pallas_call(
        functools.partial(_ffn_kernel, seq // FFN_TM),
        grid=(m // FFN_TM, nj),
        in_specs=[
            pl.BlockSpec((FFN_TM, D_MODEL), lambda i, j: (i, 0)),
            pl.BlockSpec((D_MODEL, FFN_TN), lambda i, j: (0, j)),
            pl.BlockSpec((D_MODEL, FFN_TN), lambda i, j: (0, nj + j)),
            pl.BlockSpec((CONV_WIDTH, FFN_TN), lambda i, j: (0, j)),
            pl.BlockSpec((CONV_WIDTH, FFN_TN), lambda i, j: (0, nj + j)),
            pl.BlockSpec((1, FFN_TN), lambda i, j: (0, j)),
            pl.BlockSpec((1, FFN_TN), lambda i, j: (0, nj + j)),
            pl.BlockSpec((FFN_TN, D_MODEL), lambda i, j: (j, 0)),
        ],
        out_specs=pl.BlockSpec((FFN_TM, D_MODEL), lambda i, j: (i, 0)),
        out_shape=jax.ShapeDtypeStruct((m, D_MODEL), F32),
        scratch_shapes=[pltpu.VMEM(halo, F32), pltpu.VMEM(halo, F32), pltpu.VMEM(slab, F32), pltpu.VMEM(slab, F32)],
        compiler_params=_params(blk),
        name="ffn",
    )(c, w_up_p, w_up_p, conv_w_p, conv_w_p, conv_b_p, conv_b_p, w_down_p)


def _ple_kernel(h_ref, f_ref, p_ref, n3_ref, wg_ref, wp_ref, nf_ref, o_ref, e_scr, pb_scr):
    def prep(r, carry):
        rows = pl.ds(pl.multiple_of(r * ROW_CHUNK, ROW_CHUNK), ROW_CHUNK)
        h2 = h_ref[rows, :] + f_ref[rows, :]
        o_ref[rows, :] = h2
        e_scr[rows, :] = _rms_rows(h2, n3_ref[...]).astype(BF16)
        pb_scr[rows, :] = p_ref[rows, :].astype(BF16)
        return carry

    lax.fori_loop(0, PLE_TM // ROW_CHUNK, prep, 0)

    e_b = e_scr[...]
    p_b = pb_scr[...]
    for n in range(D_MODEL // PLE_TN):
        cols = slice(n * PLE_TN, (n + 1) * PLE_TN)
        gl = jnp.dot(e_b, wg_ref[:, cols], preferred_element_type=F32)
        pp = jnp.dot(p_b, wp_ref[:, cols], preferred_element_type=F32)
        o_ref[:, cols] += pp / (1.0 + jnp.exp(-gl))

    def fin(r, carry):
        rows = pl.ds(pl.multiple_of(r * ROW_CHUNK, ROW_CHUNK), ROW_CHUNK)
        o_ref[rows, :] = _rms_rows(o_ref[rows, :], nf_ref[...])
        return carry

    lax.fori_loop(0, PLE_TM // ROW_CHUNK, fin, 0)


def _ple(h1, f, p2, norm3_w, ple_gate_b, ple_proj_b, final_norm_w):
    m = h1.shape[0]
    ple_dim = p2.shape[1]
    blk = (2 * (3 * PLE_TM * D_MODEL * 4 + PLE_TM * ple_dim * 4) + D_MODEL * D_MODEL * 2 + ple_dim * D_MODEL * 2
           + PLE_TM * D_MODEL * 2 + PLE_TM * ple_dim * 2 + 4 * PLE_TM * PLE_TN * 4)
    row_spec = pl.BlockSpec((PLE_TM, D_MODEL), lambda i, j: (i, 0))
    return pl.pallas_call(
        _ple_kernel,
        grid=(m // PLE_TM, 1),
        in_specs=[
            row_spec,
            row_spec,
            pl.BlockSpec((PLE_TM, ple_dim), lambda i, j: (i, 0)),
            _resident((1, D_MODEL)),
            _resident((D_MODEL, D_MODEL)),
            _resident((ple_dim, D_MODEL)),
            _resident((1, D_MODEL)),
        ],
        out_specs=row_spec,
        out_shape=jax.ShapeDtypeStruct((m, D_MODEL), F32),
        scratch_shapes=[pltpu.VMEM((PLE_TM, D_MODEL), BF16), pltpu.VMEM((PLE_TM, ple_dim), BF16)],
        compiler_params=_params(blk),
        name="ple",
    )(h1, f, p2, norm3_w, ple_gate_b, ple_proj_b, final_norm_w)


def _pad_halves(a):
    lead = a.shape[:-1]
    halves = a.reshape(lead + (2, D_FF))
    halves = jnp.pad(halves, [(0, 0)] * (len(lead) + 1) + [(0, D_FF_PAD - D_FF)])
    return halves.reshape(lead + (2 * D_FF_PAD,))


def kernel(x, p, norm1_w, w_in, pool_w, pool_scale, ret_gn_w, w_out, norm2_w, w_up, conv_w, conv_b, w_down,
           norm3_w, ple_gate_w, ple_proj_w, final_norm_w):
    batch, seq, d_model = x.shape
    assert w_in.shape[0] == 1, "one layer per call"
    assert d_model == D_MODEL and seq % RET_CHUNK == 0 and seq % FFN_TM == 0
    m = batch * seq
    assert m % IN_TM == 0 and m % OUT_TM == 0 and m % PLE_TM == 0
    row = lambda v: v.reshape(1, -1)

    w_in_b = w_in[0].astype(BF16)
    pool_w_b = pool_w[0].astype(BF16)
    w_out_b = w_out[0].astype(BF16)
    w_up_p = _pad_halves(w_up[0]).astype(BF16)
    conv_w_p = _pad_halves(conv_w[0])
    conv_b_p = row(_pad_halves(conv_b[0]))
    w_down_p = jnp.pad(w_down[0], ((0, D_FF_PAD - D_FF), (0, 0))).astype(BF16)
    ple_gate_b = ple_gate_w[0].astype(BF16)
    ple_proj_b = ple_proj_w[0].astype(BF16)

    x2 = x.reshape(m, D_MODEL)
    z = _in_proj(x2, row(norm1_w[0]), w_in_b)
    mix = _mixer(z, pool_w_b, row(pool_scale[0]), row(ret_gn_w[0]), batch, seq)
    h1, c = _out_proj(mix, w_out_b, x2, row(norm2_w[0]))
    f = _ffn(c, w_up_p, conv_w_p, conv_b_p, w_down_p, seq)
    out = _ple(h1, f, p[0].reshape(m, -1), row(norm3_w[0]), ple_gate_b, ple_proj_b, row(final_norm_w))
    return out.reshape(batch, seq, D_MODEL)
```

```python
import functools

import numpy as np
import jax
import jax.numpy as jnp
from jax import lax
from jax.experimental import pallas as pl
from jax.experimental.pallas import tpu as pltpu

D_MODEL = 2048
POOL_WIDTH = 1024
N_POOL_GROUPS = 4
POOL_GROUP_DIM = 256
POOL_WINDOWS = (2, 4, 8, 16)
RET_WIDTH = 1024
RET_HEADS = 8
RET_HEAD_DIM = 128
ROPE_BASE = 10000.0
IN_COLS = POOL_WIDTH + 4 * RET_WIDTH
D_FF = 5504
CONV_WIDTH = 3
NORM_EPS = 1e-6
GN_EPS = 1e-5

V7X_LANES = 128
V7X_SUBLANES = 8
V7X_VMEM_BYTES = 64 * 1024 * 1024

RET_CHUNK = 128
MIX_CHUNKS = 4
D_FF_PAD = 5632
FFN_TN = 512
FFN_TM = 1024
SUB = 256
IN_TM = 512
OUT_TM = 512
PLE_TM = 512
PLE_TN = 512
ROW_CHUNK = 128

BF16 = jnp.bfloat16
F32 = jnp.float32


def _vmem_limit(block_bytes):
    return int(min(V7X_VMEM_BYTES - 6 * 1024 * 1024, block_bytes + 16 * 1024 * 1024))


def _params(block_bytes):
    return pltpu.CompilerParams(
        dimension_semantics=("arbitrary", "arbitrary"),
        vmem_limit_bytes=_vmem_limit(block_bytes),
    )


def _resident(shape):
    return pl.BlockSpec(shape, lambda i, j: (0,) * len(shape), pipeline_mode=pl.Buffered(1))


def _rms_rows(x, w):
    ms = jnp.mean(x * x, axis=-1, keepdims=True)
    return x * lax.rsqrt(ms + NORM_EPS) * w


def _in_proj_kernel(x_ref, nw_ref, w_ref, z_ref, a_scr):
    n_sub = IN_TM // SUB
    for s in range(n_sub):
        rows = slice(s * SUB, (s + 1) * SUB)
        a_scr[rows, :] = _rms_rows(x_ref[rows, :], nw_ref[...]).astype(BF16)
    for s in range(n_sub):
        rows = slice(s * SUB, (s + 1) * SUB)
        z_ref[rows, :] = jnp.dot(a_scr[rows, :], w_ref[...], preferred_element_type=F32).astype(BF16)


def _in_proj(x2, norm1_w, w_in_b):
    m = x2.shape[0]
    blk = (2 * (IN_TM * D_MODEL * 4 + IN_TM * IN_COLS * 2) + D_MODEL * IN_COLS * 2 + IN_TM * D_MODEL * 2)
    return pl.pallas_call(
        _in_proj_kernel,
        grid=(m // IN_TM, 1),
        in_specs=[
            pl.BlockSpec((IN_TM, D_MODEL), lambda i, j: (i, 0)),
            _resident((1, D_MODEL)),
            _resident((D_MODEL, IN_COLS)),
        ],
        out_specs=pl.BlockSpec((IN_TM, IN_COLS), lambda i, j: (i, 0)),
        out_shape=jax.ShapeDtypeStruct((m, IN_COLS), BF16),
        scratch_shapes=[pltpu.VMEM((IN_TM, D_MODEL), BF16)],
        compiler_params=_params(blk),
        name="in_proj",
    )(x2, norm1_w, w_in_b)


def _mixer_tables(seq):
    c = RET_CHUNK
    pos = np.arange(seq, dtype=np.float64)
    inv_freq = 1.0 / (ROPE_BASE ** (np.arange(0, RET_HEAD_DIM, 2, dtype=np.float64) / RET_HEAD_DIM))
    ang = pos[:, None] * inv_freq[None, :]
    cos, sin = np.cos(ang), np.sin(ang)
    k_scale = RET_HEAD_DIM ** -0.5
    cc = np.concatenate([cos, cos], axis=1)
    ss = np.concatenate([-sin, sin], axis=1)
    rope = np.concatenate([cc, ss, cc * k_scale, ss * k_scale], axis=1)

    gamma = 1.0 - np.exp2(-5.0 - np.arange(RET_HEADS, dtype=np.float64))
    idx = np.arange(c, dtype=np.float64)
    diff = idx[:, None] - idx[None, :]
    decay = np.where(diff[None] >= 0, gamma[:, None, None] ** np.maximum(diff, 0.0)[None], 0.0)
    xi = gamma[None, :] ** (idx[:, None] + 1.0)
    zeta = gamma[None, :] ** (c - 1.0 - idx[:, None])
    xi_t = np.repeat(xi, RET_HEAD_DIM, axis=1)
    zeta_t = np.repeat(zeta, RET_HEAD_DIM, axis=1)
    g_chunk = tuple(float(g) for g in gamma ** c)

    band = np.zeros((N_POOL_GROUPS, c, 2 * c), dtype=np.float32)
    for g, w in enumerate(POOL_WINDOWS):
        for m_ in range(w):
            band[g, np.arange(c), c + np.arange(c) - m_] = 1.0
    t1 = np.concatenate([idx + 1.0, np.full(c, float(max(POOL_WINDOWS)))])
    inv_cnt = np.concatenate([np.repeat(1.0 / np.minimum(t1, float(w))[:, None], POOL_GROUP_DIM, axis=1)
                              for w in POOL_WINDOWS], axis=1)
    return (jnp.asarray(rope, F32), jnp.asarray(decay, F32), jnp.asarray(xi_t, F32),
            jnp.asarray(zeta_t, F32), jnp.asarray(band, BF16), jnp.asarray(inv_cnt, F32), g_chunk)


def _mixer_kernel(g_chunk, z_ref, rope_ref, decay_ref, xi_ref, zeta_ref, band_ref, inv_ref, pw_ref, ps_ref,
                  gn_ref, o_ref, uprev_scr, state_scr):
    c = RET_CHUNK
    hd = RET_HEAD_DIM
    ci = pl.program_id(1)
    heads = range(RET_HEADS)
    groups = range(N_POOL_GROUPS)
    nt = (((1,), (1,)), ((), ()))
    tn = (((0,), (0,)), ((), ()))

    @pl.when(ci == 0)
    def _():
        uprev_scr[...] = jnp.zeros_like(uprev_scr)
        state_scr[...] = jnp.zeros_like(state_scr)

    uprev_scr[c:, :] = z_ref[0:c, :POOL_WIDTH]
    st = [state_scr[h] for h in heads]

    for t in range(MIX_CHUNKS):
        rows = slice(t * c, (t + 1) * c)

        def zcols(part, h, rows=rows):
            lo = POOL_WIDTH + part * RET_WIDTH + h * hd
            return z_ref[rows, lo:lo + hd]

        qcc, qss = rope_ref[rows, 0:hd], rope_ref[rows, hd:2 * hd]
        kcc, kss = rope_ref[rows, 2 * hd:3 * hd], rope_ref[rows, 3 * hd:4 * hd]
        q_b, k_b, k_z = [], [], []
        for h in heads:
            q = zcols(0, h).astype(F32)
            k = zcols(1, h).astype(F32)
            q_r = q * qcc + pltpu.roll(q, hd // 2, axis=1) * qss
            k_r = k * kcc + pltpu.roll(k, hd // 2, axis=1) * kss
            q_b.append(q_r.astype(BF16))
            k_b.append(k_r.astype(BF16))
            k_z.append((k_r * zeta_ref[:, h * hd:(h + 1) * hd]).astype(BF16))

        def ucat(g, t=t):
            cols = slice(g * POOL_GROUP_DIM, (g + 1) * POOL_GROUP_DIM)
            return uprev_scr[:, cols] if t == 0 else z_ref[(t - 1) * c:(t + 1) * c, cols]

        sc = [lax.dot_general(q_b[h], k_b[h], nt, preferred_element_type=F32) for h in heads]
        cross = [jnp.dot(q_b[h], st[h].astype(BF16), preferred_element_type=F32) for h in heads]
        kv = [lax.dot_general(k_z[h], zcols(2, h), tn, preferred_element_type=F32) for h in heads]
        wsum = [jnp.dot(band_ref[g], ucat(g), preferred_element_type=F32) for g in groups]

        p_b = [(sc[h] * decay_ref[h]).astype(BF16) for h in heads]
        st = [g_chunk[h] * st[h] + kv[h] for h in heads]
        d_b = []
        for g, w in enumerate(POOL_WINDOWS):
            cols = slice(g * POOL_GROUP_DIM, (g + 1) * POOL_GROUP_DIM)
            inv = inv_ref[:, cols] if t == 0 else 1.0 / w
            d_b.append((wsum[g] * inv - z_ref[rows, cols].astype(F32)).astype(BF16))

        intra = [jnp.dot(p_b[h], zcols(2, h), preferred_element_type=F32) for h in heads]
        y = [jnp.dot(d_b[g], pw_ref[g], preferred_element_type=F32) for g in groups]

        for g in groups:
            cols = slice(g * POOL_GROUP_DIM, (g + 1) * POOL_GROUP_DIM)
            o_ref[rows, cols] = (y[g] * ps_ref[:, cols]).astype(BF16)
        for h in heads:
            hc = slice(h * hd, (h + 1) * hd)
            r = intra[h] + cross[h] * xi_ref[:, hc]
            mu = jnp.mean(r, axis=-1, keepdims=True)
            xc = r - mu
            var = jnp.mean(xc * xc, axis=-1, keepdims=True)
            rn = xc * lax.rsqrt(var + GN_EPS) * gn_ref[:, hc]
            gt = zcols(3, h).astype(F32)
            silu = gt / (1.0 + jnp.exp(-gt))
            o_ref[rows, POOL_WIDTH + h * hd:POOL_WIDTH + (h + 1) * hd] = (silu * rn).astype(BF16)

    for h in heads:
        state_scr[h] = st[h]
    uprev_scr[:c, :] = z_ref[(MIX_CHUNKS - 1) * c:MIX_CHUNKS * c, :POOL_WIDTH]


def _mixer(z, pool_w_b, pool_scale, ret_gn_w, batch, seq):
    c = RET_CHUNK
    rows = MIX_CHUNKS * c
    n_blocks = seq // rows
    rope, decay, xi_t, zeta_t, band, inv_cnt, g_chunk = _mixer_tables(seq)
    blk = (2 * (rows * IN_COLS * 2 + c * POOL_WIDTH * 4 + rows * 4 * RET_HEAD_DIM * 4 + rows * D_MODEL * 2)
           + RET_HEADS * c * c * 4 + 2 * c * RET_WIDTH * 4 + N_POOL_GROUPS * c * 2 * c * 2
           + N_POOL_GROUPS * POOL_GROUP_DIM * POOL_GROUP_DIM * 2
           + 2 * c * POOL_WIDTH * 2 + RET_HEADS * RET_HEAD_DIM * RET_HEAD_DIM * 4)
    return pl.pallas_call(
        functools.partial(_mixer_kernel, g_chunk),
        grid=(batch, n_blocks),
        in_specs=[
            pl.BlockSpec((rows, IN_COLS), lambda b, ci: (b * n_blocks + ci, 0)),
            pl.BlockSpec((rows, 4 * RET_HEAD_DIM), lambda b, ci: (ci, 0)),
            _resident((RET_HEADS, c, c)),
            _resident((c, RET_WIDTH)),
            _resident((c, RET_WIDTH)),
            _resident((N_POOL_GROUPS, c, 2 * c)),
            pl.BlockSpec((c, POOL_WIDTH), lambda b, ci: (jnp.minimum(ci, 1), 0)),
            _resident((N_POOL_GROUPS, POOL_GROUP_DIM, POOL_GROUP_DIM)),
            _resident((1, POOL_WIDTH)),
            _resident((1, RET_WIDTH)),
        ],
        out_specs=pl.BlockSpec((rows, D_MODEL), lambda b, ci: (b * n_blocks + ci, 0)),
        out_shape=jax.ShapeDtypeStruct((batch * seq, D_MODEL), BF16),
        scratch_shapes=[
            pltpu.VMEM((2 * c, POOL_WIDTH), BF16),
            pltpu.VMEM((RET_HEADS, RET_HEAD_DIM, RET_HEAD_DIM), F32),
        ],
        compiler_params=_params(blk),
        name="mixer",
    )(z, rope, decay, xi_t, zeta_t, band, inv_cnt, pool_w_b, pool_scale, ret_gn_w)


def _out_proj_kernel(mix_ref, w_ref, x_ref, nw_ref, h_ref, c_ref):
    n_sub = OUT_TM // SUB
    acc = [jnp.dot(mix_ref[s * SUB:(s + 1) * SUB, :], w_ref[...], preferred_element_type=F32)
           for s in range(n_sub)]
    for s in range(n_sub):
        rows = slice(s * SUB, (s + 1) * SUB)
        h = x_ref[rows, :] + acc[s]
        h_ref[rows, :] = h
        c_ref[rows, :] = _rms_rows(h, nw_ref[...]).astype(BF16)


def _out_proj(mix, w_out_b, x2, norm2_w):
    m = x2.shape[0]
    blk = 2 * (OUT_TM * D_MODEL * 2 + 2 * OUT_TM * D_MODEL * 4 + OUT_TM * D_MODEL * 2) + D_MODEL * D_MODEL * 2
    row_spec = pl.BlockSpec((OUT_TM, D_MODEL), lambda i, j: (i, 0))
    return pl.pallas_call(
        _out_proj_kernel,
        grid=(m // OUT_TM, 1),
        in_specs=[row_spec, _resident((D_MODEL, D_MODEL)), row_spec, _resident((1, D_MODEL))],
        out_specs=[row_spec, row_spec],
        out_shape=[jax.ShapeDtypeStruct((m, D_MODEL), F32), jax.ShapeDtypeStruct((m, D_MODEL), BF16)],
        compiler_params=_params(blk),
        name="out_proj",
    )(mix, w_out_b, x2, norm2_w)


FFN_SLABS = FFN_TN // V7X_LANES


def _ffn_kernel(tiles_per_seq, c_ref, wg_ref, wv_ref, cwg_ref, cwv_ref, cbg_ref, cbv_ref, wd_ref,
                o_ref, hg_scr, hv_scr, ug_scr, uv_scr):
    i = pl.program_id(0)
    j = pl.program_id(1)
    hh = V7X_SUBLANES

    @pl.when(jnp.logical_and(i == 0, j == 0))
    def _():
        hg_scr[...] = jnp.zeros_like(hg_scr)
        hv_scr[...] = jnp.zeros_like(hv_scr)

    @pl.when(j == 0)
    def _():
        o_ref[...] = jnp.zeros_like(o_ref)

    ug_scr[:, 0:hh, :] = hg_scr[j]
    uv_scr[:, 0:hh, :] = hv_scr[j]

    def up_store(s):
        c_b = c_ref[s * SUB:(s + 1) * SUB, :]
        base = hh + s * SUB
        ug = jnp.dot(c_b, wg_ref[...], preferred_element_type=F32)
        uv = jnp.dot(c_b, wv_ref[...], preferred_element_type=F32)
        for ct in range(FFN_SLABS):
            lanes = slice(ct * V7X_LANES, (ct + 1) * V7X_LANES)
            ug_scr[ct, base:base + SUB, :] = ug[:, lanes]
            uv_scr[ct, base:base + SUB, :] = uv[:, lanes]

    def conv(u_scr, s, cw, cb):
        base = hh + s * SUB
        outs = []
        for ct in range(FFN_SLABS):
            lanes = slice(ct * V7X_LANES, (ct + 1) * V7X_LANES)
            out = cb[:, lanes] + cw[CONV_WIDTH - 1:CONV_WIDTH, lanes] * u_scr[ct, base:base + SUB, :]
            for k in range(1, CONV_WIDTH):
                out = out + cw[CONV_WIDTH - 1 - k:CONV_WIDTH - k, lanes] * u_scr[ct, base - k:base - k + SUB, :]
            outs.append(out)
        return jnp.concatenate(outs, axis=1)

    n_sub = FFN_TM // SUB
    up_store(0)
    for s in range(n_sub):
        if s + 1 < n_sub:
            up_store(s + 1)
        gate = conv(ug_scr, s, cwg_ref[...], cbg_ref[...])
        val = conv(uv_scr, s, cwv_ref[...], cbv_ref[...])
        act = (gate / (1.0 + jnp.exp(-gate)) * val).astype(BF16)
        o_ref[s * SUB:(s + 1) * SUB, :] += jnp.dot(act, wd_ref[...], preferred_element_type=F32)
    keep = (i + 1) % tiles_per_seq != 0
    hg_scr[j] = jnp.where(keep, ug_scr[:, FFN_TM:FFN_TM + hh, :], 0.0)
    hv_scr[j] = jnp.where(keep, uv_scr[:, FFN_TM:FFN_TM + hh, :], 0.0)


def _ffn(c, w_up_gv, conv_w_gv, conv_b_gv, w_down_p, seq):
    m = c.shape[0]
    nj = D_FF_PAD // FFN_TN
    halo = (nj, FFN_SLABS, V7X_SUBLANES, V7X_LANES)
    slab = (FFN_SLABS, V7X_SUBLANES + FFN_TM, V7X_LANES)
    blk = (2 * (FFN_TM * D_MODEL * 2 + FFN_TM * D_MODEL * 4 + 2 * D_MODEL * FFN_TN * 2 + FFN_TN * D_MODEL * 2)
           + 2 * int(np.prod(halo)) * 4 + 2 * int(np.prod(slab)) * 4)
    return pl.pallas_call(
        functools.partial(_ffn_kernel, seq // FFN_TM),
        grid=(m // FFN_TM, nj),
        in_specs=[
            pl.BlockSpec((FFN_TM, D_MODEL), lambda i, j: (i, 0)),
            pl.BlockSpec((D_MODEL, FFN_TN), lambda i, j: (0, j)),
            pl.BlockSpec((D_MODEL, FFN_TN), lambda i, j: (0, j)),
            pl.BlockSpec((CONV_WIDTH, FFN_TN), lambda i, j: (0, j)),
            pl.BlockSpec((CONV_WIDTH, FFN_TN), lambda i, j: (0, j)),
            pl.BlockSpec((1, FFN_TN), lambda i, j: (0, j)),
            pl.BlockSpec((1, FFN_TN), lambda i, j: (0, j)),
            pl.BlockSpec((FFN_TN, D_MODEL), lambda i, j: (j, 0)),
        ],
        out_specs=pl.BlockSpec((FFN_TM, D_MODEL), lambda i, j: (i, 0)),
        out_shape=jax.ShapeDtypeStruct((m, D_MODEL), F32),
        scratch_shapes=[pltpu.VMEM(halo, F32), pltpu.VMEM(halo, F32), pltpu.VMEM(slab, F32), pltpu.VMEM(slab, F32)],
        compiler_params=_params(blk),
        name="ffn",
    )(c, *w_up_gv, *conv_w_gv, *conv_b_gv, w_down_p)


def _ple_kernel(h_ref, f_ref, p_ref, n3_ref, wg_ref, wp_ref, nf_ref, o_ref, e_scr, pb_scr):
    def prep(r, carry):
        rows = pl.ds(pl.multiple_of(r * ROW_CHUNK, ROW_CHUNK), ROW_CHUNK)
        h2 = h_ref[rows, :] + f_ref[rows, :]
        o_ref[rows, :] = h2
        e_scr[rows, :] = _rms_rows(h2, n3_ref[...]).astype(BF16)
        pb_scr[rows, :] = p_ref[rows, :].astype(BF16)
        return carry

    lax.fori_loop(0, PLE_TM // ROW_CHUNK, prep, 0)

    e_b = e_scr[...]
    p_b = pb_scr[...]
    for n in range(D_MODEL // PLE_TN):
        cols = slice(n * PLE_TN, (n + 1) * PLE_TN)
        gl = jnp.dot(e_b, wg_ref[:, cols], preferred_element_type=F32)
        pp = jnp.dot(p_b, wp_ref[:, cols], preferred_element_type=F32)
        o_ref[:, cols] += pp / (1.0 + jnp.exp(-gl))

    def fin(r, carry):
        rows = pl.ds(pl.multiple_of(r * ROW_CHUNK, ROW_CHUNK), ROW_CHUNK)
        o_ref[rows, :] = _rms_rows(o_ref[rows, :], nf_ref[...])
        return carry

    lax.fori_loop(0, PLE_TM // ROW_CHUNK, fin, 0)


def _ple(h1, f, p2, norm3_w, ple_gate_b, ple_proj_b, final_norm_w):
    m = h1.shape[0]
    ple_dim = p2.shape[1]
    blk = (2 * (3 * PLE_TM * D_MODEL * 4 + PLE_TM * ple_dim * 4) + D_MODEL * D_MODEL * 2 + ple_dim * D_MODEL * 2
           + PLE_TM * D_MODEL * 2 + PLE_TM * ple_dim * 2 + 4 * PLE_TM * PLE_TN * 4)
    row_spec = pl.BlockSpec((PLE_TM, D_MODEL), lambda i, j: (i, 0))
    return pl.pallas_call(
        _ple_kernel,
        grid=(m // PLE_TM, 1),
        in_specs=[
            row_spec,
            row_spec,
            pl.BlockSpec((PLE_TM, ple_dim), lambda i, j: (i, 0)),
            _resident((1, D_MODEL)),
            _resident((D_MODEL, D_MODEL)),
            _resident((ple_dim, D_MODEL)),
            _resident((1, D_MODEL)),
        ],
        out_specs=row_spec,
        out_shape=jax.ShapeDtypeStruct((m, D_MODEL), F32),
        scratch_shapes=[pltpu.VMEM((PLE_TM, D_MODEL), BF16), pltpu.VMEM((PLE_TM, ple_dim), BF16)],
        compiler_params=_params(blk),
        name="ple",
    )(h1, f, p2, norm3_w, ple_gate_b, ple_proj_b, final_norm_w)


def _gate_val(a, dtype):
    pad = [(0, 0)] * (a.ndim - 1) + [(0, D_FF_PAD - D_FF)]
    return (jnp.pad(a[..., :D_FF], pad).astype(dtype), jnp.pad(a[..., D_FF:], pad).astype(dtype))


def kernel(x, p, norm1_w, w_in, pool_w, pool_scale, ret_gn_w, w_out, norm2_w, w_up, conv_w, conv_b, w_down,
           norm3_w, ple_gate_w, ple_proj_w, final_norm_w):
    batch, seq, d_model = x.shape
    assert w_in.shape[0] == 1, "one layer per call"
    assert d_model == D_MODEL and seq % (MIX_CHUNKS * RET_CHUNK) == 0 and seq % FFN_TM == 0
    m = batch * seq
    assert m % IN_TM == 0 and m % OUT_TM == 0 and m % PLE_TM == 0
    row = lambda v: v.reshape(1, -1)

    w_in_b = w_in[0].astype(BF16)
    pool_w_b = pool_w[0].astype(BF16)
    w_out_b = w_out[0].astype(BF16)
    w_up_gv = _gate_val(w_up[0], BF16)
    conv_w_gv = _gate_val(conv_w[0], F32)
    conv_b_gv = _gate_val(row(conv_b[0]), F32)
    w_down_p = jnp.pad(w_down[0], ((0, D_FF_PAD - D_FF), (0, 0))).astype(BF16)
    ple_gate_b = ple_gate_w[0].astype(BF16)
    ple_proj_b = ple_proj_w[0].astype(BF16)

    x2 = x.reshape(m, D_MODEL)
    z = _in_proj(x2, row(norm1_w[0]), w_in_b)
    mix = _mixer(z, pool_w_b, row(pool_scale[0]), row(ret_gn_w[0]), batch, seq)
    h1, c = _out_proj(mix, w_out_b, x2, row(norm2_w[0]))
    f = _ffn(c, w_up_gv, conv_w_gv, conv_b_gv, w_down_p, seq)
    out = _ple(h1, f, p[0].reshape(m, -1), row(norm3_w[0]), ple_gate_b, ple_proj_b, row(final_norm_w))
    return out.reshape(batch, seq, D_MODEL)
```

```python
import functools

import numpy as np
import jax
import jax.numpy as jnp
from jax import lax
from jax.experimental import pallas as pl
from jax.experimental.pallas import tpu as pltpu

D_MODEL = 2048
POOL_WIDTH = 1024
N_POOL_GROUPS = 4
POOL_GROUP_DIM = 256
POOL_WINDOWS = (2, 4, 8, 16)
RET_WIDTH = 1024
RET_HEADS = 8
RET_HEAD_DIM = 128
ROPE_BASE = 10000.0
IN_COLS = POOL_WIDTH + 4 * RET_WIDTH
D_FF = 5504
CONV_WIDTH = 3
NORM_EPS = 1e-6
GN_EPS = 1e-5

V7X_LANES = 128
V7X_SUBLANES = 8
V7X_VMEM_BYTES = 64 * 1024 * 1024

RET_CHUNK = 128
MIX_CHUNKS = 4
D_FF_PAD = 5632
FFN_TN = 512
FFN_TM = 1024
SUB = 256
IN_TM = 512
OUT_TM = 512
PLE_TM = 512
PLE_TN = 512
STAGE_ROWS = 128

BF16 = jnp.bfloat16
F32 = jnp.float32


def _vmem_limit(block_bytes):
    return int(min(V7X_VMEM_BYTES - 6 * 1024 * 1024, block_bytes + 16 * 1024 * 1024))


def _params(block_bytes):
    return pltpu.CompilerParams(
        dimension_semantics=("arbitrary", "arbitrary"),
        vmem_limit_bytes=_vmem_limit(block_bytes),
    )


def _resident(shape):
    return pl.BlockSpec(shape, lambda i, j: (0,) * len(shape), pipeline_mode=pl.Buffered(1))


def _rms_rows(x, w):
    ms = jnp.mean(x * x, axis=-1, keepdims=True)
    return x * lax.rsqrt(ms + NORM_EPS) * w


def _in_proj_kernel(x_ref, nw_ref, w_ref, z_ref, a_scr):
    n_sub = IN_TM // SUB
    for s in range(n_sub):
        rows = slice(s * SUB, (s + 1) * SUB)
        a_scr[rows, :] = _rms_rows(x_ref[rows, :], nw_ref[...]).astype(BF16)
    for s in range(n_sub):
        rows = slice(s * SUB, (s + 1) * SUB)
        z_ref[rows, :] = jnp.dot(a_scr[rows, :], w_ref[...], preferred_element_type=F32).astype(BF16)


def _in_proj(x2, norm1_w, w_in_b):
    m = x2.shape[0]
    blk = (2 * (IN_TM * D_MODEL * 4 + IN_TM * IN_COLS * 2) + D_MODEL * IN_COLS * 2 + IN_TM * D_MODEL * 2)
    return pl.pallas_call(
        _in_proj_kernel,
        grid=(m // IN_TM, 1),
        in_specs=[
            pl.BlockSpec((IN_TM, D_MODEL), lambda i, j: (i, 0)),
            _resident((1, D_MODEL)),
            _resident((D_MODEL, IN_COLS)),
        ],
        out_specs=pl.BlockSpec((IN_TM, IN_COLS), lambda i, j: (i, 0)),
        out_shape=jax.ShapeDtypeStruct((m, IN_COLS), BF16),
        scratch_shapes=[pltpu.VMEM((IN_TM, D_MODEL), BF16)],
        compiler_params=_params(blk),
        name="in_proj",
    )(x2, norm1_w, w_in_b)


def _mixer_tables(seq):
    c = RET_CHUNK
    pos = np.arange(seq, dtype=np.float64)
    inv_freq = 1.0 / (ROPE_BASE ** (np.arange(0, RET_HEAD_DIM, 2, dtype=np.float64) / RET_HEAD_DIM))
    ang = pos[:, None] * inv_freq[None, :]
    cos, sin = np.cos(ang), np.sin(ang)
    k_scale = RET_HEAD_DIM ** -0.5
    cc = np.concatenate([cos, cos], axis=1)
    ss = np.concatenate([-sin, sin], axis=1)
    rope = np.concatenate([cc, ss, cc * k_scale, ss * k_scale], axis=1)

    gamma = 1.0 - np.exp2(-5.0 - np.arange(RET_HEADS, dtype=np.float64))
    idx = np.arange(c, dtype=np.float64)
    diff = idx[:, None] - idx[None, :]
    decay = np.where(diff[None] >= 0, gamma[:, None, None] ** np.maximum(diff, 0.0)[None], 0.0)
    xi = gamma[None, :] ** (idx[:, None] + 1.0)
    zeta = gamma[None, :] ** (c - 1.0 - idx[:, None])
    xi_t = np.repeat(xi, RET_HEAD_DIM, axis=1)
    zeta_t = np.repeat(zeta, RET_HEAD_DIM, axis=1)
    g_chunk = tuple(float(g) for g in gamma ** c)

    band = np.zeros((N_POOL_GROUPS, c, 2 * c), dtype=np.float32)
    for g, w in enumerate(POOL_WINDOWS):
        for m_ in range(w):
            band[g, np.arange(c), c + np.arange(c) - m_] = 1.0
    t1 = np.concatenate([idx + 1.0, np.full(c, float(max(POOL_WINDOWS)))])
    inv_cnt = np.concatenate([np.repeat(1.0 / np.minimum(t1, float(w))[:, None], POOL_GROUP_DIM, axis=1)
                              for w in POOL_WINDOWS], axis=1)
    return (jnp.asarray(rope, F32), jnp.asarray(decay, F32), jnp.asarray(xi_t, F32),
            jnp.asarray(zeta_t, F32), jnp.asarray(band, BF16), jnp.asarray(inv_cnt, F32), g_chunk)


def _mixer_kernel(g_chunk, z_ref, rope_ref, decay_ref, xi_ref, zeta_ref, band_ref, inv_ref, pw_ref, ps_ref,
                  gn_ref, o_ref, uprev_scr, state_scr):
    c = RET_CHUNK
    hd = RET_HEAD_DIM
    ci = pl.program_id(1)
    heads = range(RET_HEADS)
    groups = range(N_POOL_GROUPS)
    nt = (((1,), (1,)), ((), ()))
    tn = (((0,), (0,)), ((), ()))

    @pl.when(ci == 0)
    def _():
        uprev_scr[...] = jnp.zeros_like(uprev_scr)
        state_scr[...] = jnp.zeros_like(state_scr)

    uprev_scr[c:, :] = z_ref[0:c, :POOL_WIDTH]
    st = [state_scr[h] for h in heads]

    for t in range(MIX_CHUNKS):
        rows = slice(t * c, (t + 1) * c)

        def zcols(part, h, rows=rows):
            lo = POOL_WIDTH + part * RET_WIDTH + h * hd
            return z_ref[rows, lo:lo + hd]

        qcc, qss = rope_ref[rows, 0:hd], rope_ref[rows, hd:2 * hd]
        kcc, kss = rope_ref[rows, 2 * hd:3 * hd], rope_ref[rows, 3 * hd:4 * hd]
        q_b, k_b, k_z = [], [], []
        for h in heads:
            q = zcols(0, h).astype(F32)
            k = zcols(1, h).astype(F32)
            q_r = q * qcc + pltpu.roll(q, hd // 2, axis=1) * qss
            k_r = k * kcc + pltpu.roll(k, hd // 2, axis=1) * kss
            q_b.append(q_r.astype(BF16))
            k_b.append(k_r.astype(BF16))
            k_z.append((k_r * zeta_ref[:, h * hd:(h + 1) * hd]).astype(BF16))

        def ucat(g, t=t):
            cols = slice(g * POOL_GROUP_DIM, (g + 1) * POOL_GROUP_DIM)
            return uprev_scr[:, cols] if t == 0 else z_ref[(t - 1) * c:(t + 1) * c, cols]

        sc = [lax.dot_general(q_b[h], k_b[h], nt, preferred_element_type=F32) for h in heads]
        cross = [jnp.dot(q_b[h], st[h].astype(BF16), preferred_element_type=F32) for h in heads]
        kv = [lax.dot_general(k_z[h], zcols(2, h), tn, preferred_element_type=F32) for h in heads]
        wsum = [jnp.dot(band_ref[g], ucat(g), preferred_element_type=F32) for g in groups]

        p_b = [(sc[h] * decay_ref[h]).astype(BF16) for h in heads]
        st = [g_chunk[h] * st[h] + kv[h] for h in heads]
        d_b = []
        for g, w in enumerate(POOL_WINDOWS):
            cols = slice(g * POOL_GROUP_DIM, (g + 1) * POOL_GROUP_DIM)
            inv = inv_ref[:, cols] if t == 0 else 1.0 / w
            d_b.append((wsum[g] * inv - z_ref[rows, cols].astype(F32)).astype(BF16))

        intra = [jnp.dot(p_b[h], zcols(2, h), preferred_element_type=F32) for h in heads]
        y = [jnp.dot(d_b[g], pw_ref[g], preferred_element_type=F32) for g in groups]

        for g in groups:
            cols = slice(g * POOL_GROUP_DIM, (g + 1) * POOL_GROUP_DIM)
            o_ref[rows, cols] = (y[g] * ps_ref[:, cols]).astype(BF16)
        for h in heads:
            hc = slice(h * hd, (h + 1) * hd)
            r = intra[h] + cross[h] * xi_ref[:, hc]
            mu = jnp.mean(r, axis=-1, keepdims=True)
            xc = r - mu
            var = jnp.mean(xc * xc, axis=-1, keepdims=True)
            rn = xc * lax.rsqrt(var + GN_EPS) * gn_ref[:, hc]
            gt = zcols(3, h).astype(F32)
            silu = gt / (1.0 + jnp.exp(-gt))
            o_ref[rows, POOL_WIDTH + h * hd:POOL_WIDTH + (h + 1) * hd] = (silu * rn).astype(BF16)

    for h in heads:
        state_scr[h] = st[h]
    uprev_scr[:c, :] = z_ref[(MIX_CHUNKS - 1) * c:MIX_CHUNKS * c, :POOL_WIDTH]


def _mixer(z, pool_w_b, pool_scale, ret_gn_w, batch, seq):
    c = RET_CHUNK
    rows = MIX_CHUNKS * c
    n_blocks = seq // rows
    rope, decay, xi_t, zeta_t, band, inv_cnt, g_chunk = _mixer_tables(seq)
    blk = (2 * (rows * IN_COLS * 2 + c * POOL_WIDTH * 4 + rows * 4 * RET_HEAD_DIM * 4 + rows * D_MODEL * 2)
           + RET_HEADS * c * c * 4 + 2 * c * RET_WIDTH * 4 + N_POOL_GROUPS * c * 2 * c * 2
           + N_POOL_GROUPS * POOL_GROUP_DIM * POOL_GROUP_DIM * 2
           + 2 * c * POOL_WIDTH * 2 + RET_HEADS * RET_HEAD_DIM * RET_HEAD_DIM * 4)
    return pl.pallas_call(
        functools.partial(_mixer_kernel, g_chunk),
        grid=(batch, n_blocks),
        in_specs=[
            pl.BlockSpec((rows, IN_COLS), lambda b, ci: (b * n_blocks + ci, 0)),
            pl.BlockSpec((rows, 4 * RET_HEAD_DIM), lambda b, ci: (ci, 0)),
            _resident((RET_HEADS, c, c)),
            _resident((c, RET_WIDTH)),
            _resident((c, RET_WIDTH)),
            _resident((N_POOL_GROUPS, c, 2 * c)),
            pl.BlockSpec((c, POOL_WIDTH), lambda b, ci: (jnp.minimum(ci, 1), 0)),
            _resident((N_POOL_GROUPS, POOL_GROUP_DIM, POOL_GROUP_DIM)),
            _resident((1, POOL_WIDTH)),
            _resident((1, RET_WIDTH)),
        ],
        out_specs=pl.BlockSpec((rows, D_MODEL), lambda b, ci: (b * n_blocks + ci, 0)),
        out_shape=jax.ShapeDtypeStruct((batch * seq, D_MODEL), BF16),
        scratch_shapes=[
            pltpu.VMEM((2 * c, POOL_WIDTH), BF16),
            pltpu.VMEM((RET_HEADS, RET_HEAD_DIM, RET_HEAD_DIM), F32),
        ],
        compiler_params=_params(blk),
        name="mixer",
    )(z, rope, decay, xi_t, zeta_t, band, inv_cnt, pool_w_b, pool_scale, ret_gn_w)


def _out_proj_kernel(mix_ref, w_ref, x_ref, nw_ref, h_ref, c_ref):
    n_sub = OUT_TM // SUB
    acc = [jnp.dot(mix_ref[s * SUB:(s + 1) * SUB, :], w_ref[...], preferred_element_type=F32)
           for s in range(n_sub)]
    for s in range(n_sub):
        rows = slice(s * SUB, (s + 1) * SUB)
        h = x_ref[rows, :] + acc[s]
        h_ref[rows, :] = h
        c_ref[rows, :] = _rms_rows(h, nw_ref[...]).astype(BF16)


def _out_proj(mix, w_out_b, x2, norm2_w):
    m = x2.shape[0]
    blk = 2 * (OUT_TM * D_MODEL * 2 + 2 * OUT_TM * D_MODEL * 4 + OUT_TM * D_MODEL * 2) + D_MODEL * D_MODEL * 2
    row_spec = pl.BlockSpec((OUT_TM, D_MODEL), lambda i, j: (i, 0))
    return pl.pallas_call(
        _out_proj_kernel,
        grid=(m // OUT_TM, 1),
        in_specs=[row_spec, _resident((D_MODEL, D_MODEL)), row_spec, _resident((1, D_MODEL))],
        out_specs=[row_spec, row_spec],
        out_shape=[jax.ShapeDtypeStruct((m, D_MODEL), F32), jax.ShapeDtypeStruct((m, D_MODEL), BF16)],
        compiler_params=_params(blk),
        name="out_proj",
    )(mix, w_out_b, x2, norm2_w)


FFN_SLABS = FFN_TN // V7X_LANES


def _ffn_kernel(tiles_per_seq, c_ref, wg_ref, wv_ref, cwg_ref, cwv_ref, cbg_ref, cbv_ref, wd_ref,
                o_ref, hg_scr, hv_scr, ug_scr, uv_scr):
    i = pl.program_id(0)
    j = pl.program_id(1)
    hh = V7X_SUBLANES

    @pl.when(jnp.logical_and(i == 0, j == 0))
    def _():
        hg_scr[...] = jnp.zeros_like(hg_scr)
        hv_scr[...] = jnp.zeros_like(hv_scr)

    @pl.when(j == 0)
    def _():
        o_ref[...] = jnp.zeros_like(o_ref)

    ug_scr[:, 0:hh, :] = hg_scr[j]
    uv_scr[:, 0:hh, :] = hv_scr[j]

    def up_store(s):
        c_b = c_ref[s * SUB:(s + 1) * SUB, :]
        base = hh + s * SUB
        ug = jnp.dot(c_b, wg_ref[...], preferred_element_type=F32)
        uv = jnp.dot(c_b, wv_ref[...], preferred_element_type=F32)
        for ct in range(FFN_SLABS):
            lanes = slice(ct * V7X_LANES, (ct + 1) * V7X_LANES)
            ug_scr[ct, base:base + SUB, :] = ug[:, lanes]
            uv_scr[ct, base:base + SUB, :] = uv[:, lanes]

    def conv(u_scr, s, cw, cb):
        base = hh + s * SUB
        outs = []
        for ct in range(FFN_SLABS):
            lanes = slice(ct * V7X_LANES, (ct + 1) * V7X_LANES)
            out = cb[:, lanes] + cw[CONV_WIDTH - 1:CONV_WIDTH, lanes] * u_scr[ct, base:base + SUB, :]
            for k in range(1, CONV_WIDTH):
                out = out + cw[CONV_WIDTH - 1 - k:CONV_WIDTH - k, lanes] * u_scr[ct, base - k:base - k + SUB, :]
            outs.append(out)
        return jnp.concatenate(outs, axis=1)

    n_sub = FFN_TM // SUB
    up_store(0)
    for s in range(n_sub):
        if s + 1 < n_sub:
            up_store(s + 1)
        gate = conv(ug_scr, s, cwg_ref[...], cbg_ref[...])
        val = conv(uv_scr, s, cwv_ref[...], cbv_ref[...])
        act = (gate / (1.0 + jnp.exp(-gate)) * val).astype(BF16)
        o_ref[s * SUB:(s + 1) * SUB, :] += jnp.dot(act, wd_ref[...], preferred_element_type=F32)
    keep = (i + 1) % tiles_per_seq != 0
    hg_scr[j] = jnp.where(keep, ug_scr[:, FFN_TM:FFN_TM + hh, :], 0.0)
    hv_scr[j] = jnp.where(keep, uv_scr[:, FFN_TM:FFN_TM + hh, :], 0.0)


def _ffn(c, w_up_gv, conv_w_gv, conv_b_gv, w_down_p, seq):
    m = c.shape[0]
    nj = D_FF_PAD // FFN_TN
    halo = (nj, FFN_SLABS, V7X_SUBLANES, V7X_LANES)
    slab = (FFN_SLABS, V7X_SUBLANES + FFN_TM, V7X_LANES)
    blk = (2 * (FFN_TM * D_MODEL * 2 + FFN_TM * D_MODEL * 4 + 2 * D_MODEL * FFN_TN * 2 + FFN_TN * D_MODEL * 2)
           + 2 * int(np.prod(halo)) * 4 + 2 * int(np.prod(slab)) * 4)
    return pl.pallas_call(
        functools.partial(_ffn_kernel, seq // FFN_TM),
        grid=(m // FFN_TM, nj),
        in_specs=[
            pl.BlockSpec((FFN_TM, D_MODEL), lambda i, j: (i, 0)),
            pl.BlockSpec((D_MODEL, FFN_TN), lambda i, j: (0, j)),
            pl.BlockSpec((D_MODEL, FFN_TN), lambda i, j: (0, j)),
            pl.BlockSpec((CONV_WIDTH, FFN_TN), lambda i, j: (0, j)),
            pl.BlockSpec((CONV_WIDTH, FFN_TN), lambda i, j: (0, j)),
            pl.BlockSpec((1, FFN_TN), lambda i, j: (0, j)),
            pl.BlockSpec((1, FFN_TN), lambda i, j: (0, j)),
            pl.BlockSpec((FFN_TN, D_MODEL), lambda i, j: (j, 0)),
        ],
        out_specs=pl.BlockSpec((FFN_TM, D_MODEL), lambda i, j: (i, 0)),
        out_shape=jax.ShapeDtypeStruct((m, D_MODEL), F32),
        scratch_shapes=[pltpu.VMEM(halo, F32), pltpu.VMEM(halo, F32), pltpu.VMEM(slab, F32), pltpu.VMEM(slab, F32)],
        compiler_params=_params(blk),
        name="ffn",
    )(c, *w_up_gv, *conv_w_gv, *conv_b_gv, w_down_p)


def _ple_kernel(h_ref, f_ref, p_ref, n3_ref, wg_ref, wp_ref, nf_ref, o_ref, e_scr, pb_scr):
    subs = [slice(s * SUB, (s + 1) * SUB) for s in range(PLE_TM // SUB)]
    chunks = [slice(n * PLE_TN, (n + 1) * PLE_TN) for n in range(D_MODEL // PLE_TN)]
    for rows in subs:
        h2 = h_ref[rows, :] + f_ref[rows, :]
        o_ref[rows, :] = h2
        e_scr[rows, :] = _rms_rows(h2, n3_ref[...]).astype(BF16)
        pb_scr[rows, :] = p_ref[rows, :].astype(BF16)
    for rows in subs:
        for cols in chunks:
            gate_logit = jnp.dot(e_scr[rows, :], wg_ref[:, cols], preferred_element_type=F32)
            emb = jnp.dot(pb_scr[rows, :], wp_ref[:, cols], preferred_element_type=F32)
            o_ref[rows, cols] += emb / (1.0 + jnp.exp(-gate_logit))
    for rows in subs:
        o_ref[rows, :] = _rms_rows(o_ref[rows, :], nf_ref[...])


def _ple(h1, f, p2, norm3_w, ple_gate_b, ple_proj_b, final_norm_w):
    m = h1.shape[0]
    ple_dim = p2.shape[1]
    blk = (2 * (3 * PLE_TM * D_MODEL * 4 + PLE_TM * ple_dim * 4) + D_MODEL * D_MODEL * 2 + ple_dim * D_MODEL * 2
           + PLE_TM * D_MODEL * 2 + PLE_TM * ple_dim * 2)
    row_spec = pl.BlockSpec((PLE_TM, D_MODEL), lambda i, j: (i, 0))
    return pl.pallas_call(
        _ple_kernel,
        grid=(m // PLE_TM, 1),
        in_specs=[
            row_spec,
            row_spec,
            pl.BlockSpec((PLE_TM, ple_dim), lambda i, j: (i, 0)),
            _resident((1, D_MODEL)),
            _resident((D_MODEL, D_MODEL)),
            _resident((ple_dim, D_MODEL)),
            _resident((1, D_MODEL)),
        ],
        out_specs=row_spec,
        out_shape=jax.ShapeDtypeStruct((m, D_MODEL), F32),
        scratch_shapes=[pltpu.VMEM((PLE_TM, D_MODEL), BF16), pltpu.VMEM((PLE_TM, ple_dim), BF16)],
        compiler_params=_params(blk),
        name="ple",
    )(h1, f, p2, norm3_w, ple_gate_b, ple_proj_b, final_norm_w)


def _gate_val(a, dtype):
    pad = [(0, 0)] * (a.ndim - 1) + [(0, D_FF_PAD - D_FF)]
    return (jnp.pad(a[..., :D_FF], pad).astype(dtype), jnp.pad(a[..., D_FF:], pad).astype(dtype))


def _stage_up_kernel(w_ref, g_ref, v_ref):
    zeros = jnp.zeros((STAGE_ROWS, D_FF_PAD - D_FF), BF16)
    g_ref[:, :D_FF] = w_ref[:, :D_FF].astype(BF16)
    g_ref[:, D_FF:] = zeros
    v_ref[:, :D_FF] = w_ref[:, D_FF:].astype(BF16)
    v_ref[:, D_FF:] = zeros


def _stage_up(w_up):
    k = w_up.shape[0]
    blk = 2 * (STAGE_ROWS * 2 * D_FF * 4 + 2 * STAGE_ROWS * D_FF_PAD * 2)
    out = jax.ShapeDtypeStruct((k, D_FF_PAD), BF16)
    return pl.pallas_call(
        _stage_up_kernel,
        grid=(k // STAGE_ROWS, 1),
        in_specs=[pl.BlockSpec((STAGE_ROWS, 2 * D_FF), lambda i, j: (i, 0))],
        out_specs=[pl.BlockSpec((STAGE_ROWS, D_FF_PAD), lambda i, j: (i, 0))] * 2,
        out_shape=[out, out],
        compiler_params=_params(blk),
        name="stage_up",
    )(w_up)


def _stage_down_kernel(w_ref, o_ref):
    row = pl.program_id(0) * FFN_TN + lax.broadcasted_iota(jnp.int32, (FFN_TN, D_MODEL), 0)
    o_ref[...] = jnp.where(row < D_FF, w_ref[...], 0.0).astype(BF16)


def _stage_down(w_down):
    blk = 2 * (FFN_TN * D_MODEL * 4 + FFN_TN * D_MODEL * 2)
    return pl.pallas_call(
        _stage_down_kernel,
        grid=(D_FF_PAD // FFN_TN, 1),
        in_specs=[pl.BlockSpec((FFN_TN, D_MODEL), lambda i, j: (i, 0))],
        out_specs=pl.BlockSpec((FFN_TN, D_MODEL), lambda i, j: (i, 0)),
        out_shape=jax.ShapeDtypeStruct((D_FF_PAD, D_MODEL), BF16),
        compiler_params=_params(blk),
        name="stage_down",
    )(w_down)


def kernel(x, p, norm1_w, w_in, pool_w, pool_scale, ret_gn_w, w_out, norm2_w, w_up, conv_w, conv_b, w_down,
           norm3_w, ple_gate_w, ple_proj_w, final_norm_w):
    batch, seq, d_model = x.shape
    assert w_in.shape[0] == 1, "one layer per call"
    assert d_model == D_MODEL and seq % (MIX_CHUNKS * RET_CHUNK) == 0 and seq % FFN_TM == 0
    m = batch * seq
    assert m % IN_TM == 0 and m % OUT_TM == 0 and m % PLE_TM == 0
    row = lambda v: v.reshape(1, -1)

    w_in_b = w_in[0].astype(BF16)
    pool_w_b = pool_w[0].astype(BF16)
    w_out_b = w_out[0].astype(BF16)
    w_up_gv = _stage_up(w_up[0])
    conv_w_gv = _gate_val(conv_w[0], F32)
    conv_b_gv = _gate_val(row(conv_b[0]), F32)
    w_down_p = _stage_down(w_down[0])
    ple_gate_b = ple_gate_w[0].astype(BF16)
    ple_proj_b = ple_proj_w[0].astype(BF16)

    x2 = x.reshape(m, D_MODEL)
    z = _in_proj(x2, row(norm1_w[0]), w_in_b)
    mix = _mixer(z, pool_w_b, row(pool_scale[0]), row(ret_gn_w[0]), batch, seq)
    h1, c = _out_proj(mix, w_out_b, x2, row(norm2_w[0]))
    f = _ffn(c, w_up_gv, conv_w_gv, conv_b_gv, w_down_p, seq)
    out = _ple(h1, f, p[0].reshape(m, -1), row(norm3_w[0]), ple_gate_b, ple_proj_b, row(final_norm_w))
    return out.reshape(batch, seq, D_MODEL)
```

```python
import functools

import numpy as np
import jax
import jax.numpy as jnp
from jax import lax
from jax.experimental import pallas as pl
from jax.experimental.pallas import tpu as pltpu

D_MODEL = 2048
POOL_WIDTH = 1024
N_POOL_GROUPS = 4
POOL_GROUP_DIM = 256
POOL_WINDOWS = (2, 4, 8, 16)
RET_WIDTH = 1024
RET_HEADS = 8
RET_HEAD_DIM = 128
ROPE_BASE = 10000.0
IN_COLS = POOL_WIDTH + 4 * RET_WIDTH
D_FF = 5504
CONV_WIDTH = 3
NORM_EPS = 1e-6
GN_EPS = 1e-5

V7X_LANES = 128
V7X_SUBLANES = 8
V7X_VMEM_BYTES = 64 * 1024 * 1024

RET_CHUNK = 128
MIX_CHUNKS = 4
D_FF_PAD = 5632
FFN_TN = 512
FFN_TM = 1024
SUB = 256
IN_TM = 512
OUT_TN = 512
PLE_TM = 512
PLE_TN = 512
STAGE_ROWS = 128

BF16 = jnp.bfloat16
F32 = jnp.float32


def _vmem_limit(block_bytes):
    return int(min(V7X_VMEM_BYTES - 6 * 1024 * 1024, block_bytes + 16 * 1024 * 1024))


def _params(block_bytes):
    return pltpu.CompilerParams(
        dimension_semantics=("arbitrary", "arbitrary"),
        vmem_limit_bytes=_vmem_limit(block_bytes),
    )


def _resident(shape):
    return pl.BlockSpec(shape, lambda i, j: (0,) * len(shape), pipeline_mode=pl.Buffered(1))


def _rms_rows(x, w):
    ms = jnp.mean(x * x, axis=-1, keepdims=True)
    return x * lax.rsqrt(ms + NORM_EPS) * w


def _in_proj_kernel(x_ref, nw_ref, w_ref, z_ref, a_scr):
    n_sub = IN_TM // SUB
    for s in range(n_sub):
        rows = slice(s * SUB, (s + 1) * SUB)
        a_scr[rows, :] = _rms_rows(x_ref[rows, :], nw_ref[...]).astype(BF16)
    for s in range(n_sub):
        rows = slice(s * SUB, (s + 1) * SUB)
        z_ref[rows, :] = jnp.dot(a_scr[rows, :], w_ref[...], preferred_element_type=F32).astype(BF16)


def _in_proj(x2, norm1_w, w_in_b):
    m = x2.shape[0]
    blk = (2 * (IN_TM * D_MODEL * 4 + IN_TM * IN_COLS * 2) + D_MODEL * IN_COLS * 2 + IN_TM * D_MODEL * 2)
    return pl.pallas_call(
        _in_proj_kernel,
        grid=(m // IN_TM, 1),
        in_specs=[
            pl.BlockSpec((IN_TM, D_MODEL), lambda i, j: (i, 0)),
            _resident((1, D_MODEL)),
            _resident((D_MODEL, IN_COLS)),
        ],
        out_specs=pl.BlockSpec((IN_TM, IN_COLS), lambda i, j: (i, 0)),
        out_shape=jax.ShapeDtypeStruct((m, IN_COLS), BF16),
        scratch_shapes=[pltpu.VMEM((IN_TM, D_MODEL), BF16)],
        compiler_params=_params(blk),
        name="in_proj",
    )(x2, norm1_w, w_in_b)


def _mixer_tables(seq):
    c = RET_CHUNK
    pos = np.arange(seq, dtype=np.float64)
    inv_freq = 1.0 / (ROPE_BASE ** (np.arange(0, RET_HEAD_DIM, 2, dtype=np.float64) / RET_HEAD_DIM))
    ang = pos[:, None] * inv_freq[None, :]
    cos, sin = np.cos(ang), np.sin(ang)
    k_scale = RET_HEAD_DIM ** -0.5
    cc = np.concatenate([cos, cos], axis=1)
    ss = np.concatenate([-sin, sin], axis=1)
    rope = np.concatenate([cc, ss, cc * k_scale, ss * k_scale], axis=1)

    gamma = 1.0 - np.exp2(-5.0 - np.arange(RET_HEADS, dtype=np.float64))
    idx = np.arange(c, dtype=np.float64)
    diff = idx[:, None] - idx[None, :]
    decay = np.where(diff[None] >= 0, gamma[:, None, None] ** np.maximum(diff, 0.0)[None], 0.0)
    xi = gamma[None, :] ** (idx[:, None] + 1.0)
    zeta = gamma[None, :] ** (c - 1.0 - idx[:, None])
    xi_t = np.repeat(xi, RET_HEAD_DIM, axis=1)
    zeta_t = np.repeat(zeta, RET_HEAD_DIM, axis=1)
    g_chunk = tuple(float(g) for g in gamma ** c)

    band = np.zeros((N_POOL_GROUPS, c, 2 * c), dtype=np.float32)
    for g, w in enumerate(POOL_WINDOWS):
        for m_ in range(w):
            band[g, np.arange(c), c + np.arange(c) - m_] = 1.0
    t1 = np.concatenate([idx + 1.0, np.full(c, float(max(POOL_WINDOWS)))])
    inv_cnt = np.concatenate([np.repeat(1.0 / np.minimum(t1, float(w))[:, None], POOL_GROUP_DIM, axis=1)
                              for w in POOL_WINDOWS], axis=1)
    return (jnp.asarray(rope, F32), jnp.asarray(decay, F32), jnp.asarray(xi_t, F32),
            jnp.asarray(zeta_t, F32), jnp.asarray(band, BF16), jnp.asarray(inv_cnt, F32), g_chunk)


def _mixer_kernel(g_chunk, n_steps, blocks_per_seq, z_ref, rope_ref, decay_ref, xi_ref, zeta_ref, band_ref, inv_ref,
                  pw_ref, ps_ref, gn_ref, x_ref, wo_ref, n2_ref, h_ref, c_ref, uprev_scr, state_scr, mix_scr):
    c = RET_CHUNK
    hd = RET_HEAD_DIM
    g = pl.program_id(0)
    ci = lax.rem(jnp.minimum(g, n_steps - 1), blocks_per_seq)
    slot = lax.rem(g, 2)
    mix_cur = mix_scr.at[slot]
    mix_prev = mix_scr.at[1 - slot]
    heads = range(RET_HEADS)
    groups = range(N_POOL_GROUPS)
    nt = (((1,), (1,)), ((), ()))
    tn = (((0,), (0,)), ((), ()))

    @pl.when(g == 0)
    def _():
        mix_scr[...] = jnp.zeros_like(mix_scr)

    @pl.when(ci == 0)
    def _():
        uprev_scr[...] = jnp.zeros_like(uprev_scr)
        state_scr[...] = jnp.zeros_like(state_scr)

    n_col = D_MODEL // OUT_TN
    pieces = [(u, n) for u in range(MIX_CHUNKS * c // SUB) for n in range(n_col)]
    assert len(pieces) == 2 * MIX_CHUNKS

    def out_piece(k):
        u, n = pieces[k]
        rows = slice(u * SUB, (u + 1) * SUB)
        cols = slice(n * OUT_TN, (n + 1) * OUT_TN)
        acc = jnp.dot(mix_prev[rows, :], wo_ref[:, cols], preferred_element_type=F32)
        h_ref[rows, cols] = x_ref[rows, cols] + acc
        if n == n_col - 1:
            c_ref[rows, :] = _rms_rows(h_ref[rows, :], n2_ref[...]).astype(BF16)

    uprev_scr[c:, :] = z_ref[0:c, :POOL_WIDTH]
    st = [state_scr[h] for h in heads]

    for t in range(MIX_CHUNKS):
        rows = slice(t * c, (t + 1) * c)

        def zcols(part, h, rows=rows):
            lo = POOL_WIDTH + part * RET_WIDTH + h * hd
            return z_ref[rows, lo:lo + hd]

        qcc, qss = rope_ref[rows, 0:hd], rope_ref[rows, hd:2 * hd]
        kcc, kss = rope_ref[rows, 2 * hd:3 * hd], rope_ref[rows, 3 * hd:4 * hd]
        q_b, q_x, k_b, k_z = [], [], [], []
        for h in heads:
            hc = slice(h * hd, (h + 1) * hd)
            q = zcols(0, h).astype(F32)
            k = zcols(1, h).astype(F32)
            q_r = q * qcc + pltpu.roll(q, hd // 2, axis=1) * qss
            k_r = k * kcc + pltpu.roll(k, hd // 2, axis=1) * kss
            q_b.append(q_r.astype(BF16))
            q_x.append((q_r * xi_ref[:, hc]).astype(BF16))
            k_b.append(k_r.astype(BF16))
            k_z.append((k_r * zeta_ref[:, hc]).astype(BF16))

        def ucat(gp, t=t):
            cols = slice(gp * POOL_GROUP_DIM, (gp + 1) * POOL_GROUP_DIM)
            return uprev_scr[:, cols] if t == 0 else z_ref[(t - 1) * c:(t + 1) * c, cols]

        out_piece(2 * t if t == 0 else 2 * t - 1)
        sc = [lax.dot_general(q_b[h], k_b[h], nt, preferred_element_type=F32) for h in heads]
        kv = [lax.dot_general(k_z[h], zcols(2, h), tn, preferred_element_type=F32) for h in heads]
        wsum = [jnp.dot(band_ref[gp], ucat(gp), preferred_element_type=F32) for gp in groups]

        lhs = [jnp.concatenate([(sc[h] * decay_ref[h]).astype(BF16), q_x[h]], axis=1) for h in heads]
        rhs = [jnp.concatenate([zcols(2, h), st[h].astype(BF16)], axis=0) for h in heads]
        st = [g_chunk[h] * st[h] + kv[h] for h in heads]
        d_b = []
        for gp, w in enumerate(POOL_WINDOWS):
            cols = slice(gp * POOL_GROUP_DIM, (gp + 1) * POOL_GROUP_DIM)
            inv = inv_ref[:, cols] if t == 0 else 1.0 / w
            d_b.append((wsum[gp] * inv - z_ref[rows, cols].astype(F32)).astype(BF16))

        if t > 0:
            out_piece(2 * t)
        ret = [jnp.dot(lhs[h], rhs[h], preferred_element_type=F32) for h in heads]
        y = [jnp.dot(d_b[gp], pw_ref[gp], preferred_element_type=F32) for gp in groups]

        for gp in groups:
            cols = slice(gp * POOL_GROUP_DIM, (gp + 1) * POOL_GROUP_DIM)
            mix_cur[rows, cols] = (y[gp] * ps_ref[:, cols]).astype(BF16)
        for h in heads:
            hc = slice(h * hd, (h + 1) * hd)
            r = ret[h]
            mu = jnp.mean(r, axis=-1, keepdims=True)
            xc = r - mu
            var = jnp.mean(xc * xc, axis=-1, keepdims=True)
            rn = xc * lax.rsqrt(var + GN_EPS) * gn_ref[:, hc]
            gt = zcols(3, h).astype(F32)
            silu = gt / (1.0 + jnp.exp(-gt))
            mix_cur[rows, POOL_WIDTH + h * hd:POOL_WIDTH + (h + 1) * hd] = (silu * rn).astype(BF16)

    out_piece(2 * MIX_CHUNKS - 1)
    for h in heads:
        state_scr[h] = st[h]
    uprev_scr[:c, :] = z_ref[(MIX_CHUNKS - 1) * c:MIX_CHUNKS * c, :POOL_WIDTH]


def _mixer(z, pool_w_b, pool_scale, ret_gn_w, x2, w_out_b, norm2_w, batch, seq):
    c = RET_CHUNK
    rows = MIX_CHUNKS * c
    blocks_per_seq = seq // rows
    n_steps = batch * blocks_per_seq
    m = batch * seq
    rope, decay, xi_t, zeta_t, band, inv_cnt, g_chunk = _mixer_tables(seq)
    blk = (2 * (rows * IN_COLS * 2 + c * POOL_WIDTH * 4 + rows * 4 * RET_HEAD_DIM * 4
                + 2 * rows * D_MODEL * 4 + rows * D_MODEL * 2)
           + RET_HEADS * c * c * 4 + 2 * c * RET_WIDTH * 4 + N_POOL_GROUPS * c * 2 * c * 2
           + N_POOL_GROUPS * POOL_GROUP_DIM * POOL_GROUP_DIM * 2 + D_MODEL * D_MODEL * 2
           + 2 * c * POOL_WIDTH * 2 + RET_HEADS * RET_HEAD_DIM * RET_HEAD_DIM * 4 + 2 * rows * D_MODEL * 2)
    cur = lambda g: jnp.minimum(g, n_steps - 1)
    prev_spec = pl.BlockSpec((rows, D_MODEL), lambda g, j: (jnp.maximum(g - 1, 0), 0))
    return pl.pallas_call(
        functools.partial(_mixer_kernel, g_chunk, n_steps, blocks_per_seq),
        grid=(n_steps + 1, 1),
        in_specs=[
            pl.BlockSpec((rows, IN_COLS), lambda g, j: (cur(g), 0)),
            pl.BlockSpec((rows, 4 * RET_HEAD_DIM), lambda g, j: (cur(g) % blocks_per_seq, 0)),
            _resident((RET_HEADS, c, c)),
            _resident((c, RET_WIDTH)),
            _resident((c, RET_WIDTH)),
            _resident((N_POOL_GROUPS, c, 2 * c)),
            pl.BlockSpec((c, POOL_WIDTH), lambda g, j: (jnp.minimum(cur(g) % blocks_per_seq, 1), 0)),
            _resident((N_POOL_GROUPS, POOL_GROUP_DIM, POOL_GROUP_DIM)),
            _resident((1, POOL_WIDTH)),
            _resident((1, RET_WIDTH)),
            prev_spec,
            _resident((D_MODEL, D_MODEL)),
            _resident((1, D_MODEL)),
        ],
        out_specs=[prev_spec, prev_spec],
        out_shape=[jax.ShapeDtypeStruct((m, D_MODEL), F32), jax.ShapeDtypeStruct((m, D_MODEL), BF16)],
        scratch_shapes=[
            pltpu.VMEM((2 * c, POOL_WIDTH), BF16),
            pltpu.VMEM((RET_HEADS, RET_HEAD_DIM, RET_HEAD_DIM), F32),
            pltpu.VMEM((2, rows, D_MODEL), BF16),
        ],
        compiler_params=_params(blk),
        name="mixer",
    )(z, rope, decay, xi_t, zeta_t, band, inv_cnt, pool_w_b, pool_scale, ret_gn_w, x2, w_out_b, norm2_w)


FFN_SLABS = FFN_TN // V7X_LANES


def _ffn_kernel(tiles_per_seq, c_ref, wg_ref, wv_ref, cwg_ref, cwv_ref, cbg_ref, cbv_ref, wd_ref,
                o_ref, hg_scr, hv_scr, ug_scr, uv_scr):
    i = pl.program_id(0)
    j = pl.program_id(1)
    hh = V7X_SUBLANES

    @pl.when(jnp.logical_and(i == 0, j == 0))
    def _():
        hg_scr[...] = jnp.zeros_like(hg_scr)
        hv_scr[...] = jnp.zeros_like(hv_scr)

    @pl.when(j == 0)
    def _():
        o_ref[...] = jnp.zeros_like(o_ref)

    ug_scr[:, 0:hh, :] = hg_scr[j]
    uv_scr[:, 0:hh, :] = hv_scr[j]

    def up_store(s):
        c_b = c_ref[s * SUB:(s + 1) * SUB, :]
        base = hh + s * SUB
        ug = jnp.dot(c_b, wg_ref[...], preferred_element_type=F32)
        uv = jnp.dot(c_b, wv_ref[...], preferred_element_type=F32)
        for ct in range(FFN_SLABS):
            lanes = slice(ct * V7X_LANES, (ct + 1) * V7X_LANES)
            ug_scr[ct, base:base + SUB, :] = ug[:, lanes]
            uv_scr[ct, base:base + SUB, :] = uv[:, lanes]

    def conv(u_scr, s, cw, cb):
        base = hh + s * SUB
        outs = []
        for ct in range(FFN_SLABS):
            lanes = slice(ct * V7X_LANES, (ct + 1) * V7X_LANES)
            out = cb[:, lanes] + cw[CONV_WIDTH - 1:CONV_WIDTH, lanes] * u_scr[ct, base:base + SUB, :]
            for k in range(1, CONV_WIDTH):
                out = out + cw[CONV_WIDTH - 1 - k:CONV_WIDTH - k, lanes] * u_scr[ct, base - k:base - k + SUB, :]
            outs.append(out)
        return jnp.concatenate(outs, axis=1)

    n_sub = FFN_TM // SUB
    up_store(0)
    for s in range(n_sub):
        if s + 1 < n_sub:
            up_store(s + 1)
        gate = conv(ug_scr, s, cwg_ref[...], cbg_ref[...])
        val = conv(uv_scr, s, cwv_ref[...], cbv_ref[...])
        act = (gate / (1.0 + jnp.exp(-gate)) * val).astype(BF16)
        o_ref[s * SUB:(s + 1) * SUB, :] += jnp.dot(act, wd_ref[...], preferred_element_type=F32)
    keep = (i + 1) % tiles_per_seq != 0
    hg_scr[j] = jnp.where(keep, ug_scr[:, FFN_TM:FFN_TM + hh, :], 0.0)
    hv_scr[j] = jnp.where(keep, uv_scr[:, FFN_TM:FFN_TM + hh, :], 0.0)


def _ffn(c, w_up_gv, conv_w_gv, conv_b_gv, w_down_p, seq):
    m = c.shape[0]
    nj = D_FF_PAD // FFN_TN
    halo = (nj, FFN_SLABS, V7X_SUBLANES, V7X_LANES)
    slab = (FFN_SLABS, V7X_SUBLANES + FFN_TM, V7X_LANES)
    blk = (2 * (FFN_TM * D_MODEL * 2 + FFN_TM * D_MODEL * 4 + 2 * D_MODEL * FFN_TN * 2 + FFN_TN * D_MODEL * 2)
           + 2 * int(np.prod(halo)) * 4 + 2 * int(np.prod(slab)) * 4)
    return pl.pallas_call(
        functools.partial(_ffn_kernel, seq // FFN_TM),
        grid=(m // FFN_TM, nj),
        in_specs=[
            pl.BlockSpec((FFN_TM, D_MODEL), lambda i, j: (i, 0)),
            pl.BlockSpec((D_MODEL, FFN_TN), lambda i, j: (0, j)),
            pl.BlockSpec((D_MODEL, FFN_TN), lambda i, j: (0, j)),
            pl.BlockSpec((CONV_WIDTH, FFN_TN), lambda i, j: (0, j)),
            pl.BlockSpec((CONV_WIDTH, FFN_TN), lambda i, j: (0, j)),
            pl.BlockSpec((1, FFN_TN), lambda i, j: (0, j)),
            pl.BlockSpec((1, FFN_TN), lambda i, j: (0, j)),
            pl.BlockSpec((FFN_TN, D_MODEL), lambda i, j: (j, 0)),
        ],
        out_specs=pl.BlockSpec((FFN_TM, D_MODEL), lambda i, j: (i, 0)),
        out_shape=jax.ShapeDtypeStruct((m, D_MODEL), F32),
        scratch_shapes=[pltpu.VMEM(halo, F32), pltpu.VMEM(halo, F32), pltpu.VMEM(slab, F32), pltpu.VMEM(slab, F32)],
        compiler_params=_params(blk),
        name="ffn",
    )(c, *w_up_gv, *conv_w_gv, *conv_b_gv, w_down_p)


def _ple_kernel(h_ref, f_ref, p_ref, n3_ref, wg_ref, wp_ref, nf_ref, o_ref, e_scr, pb_scr):
    subs = [slice(s * SUB, (s + 1) * SUB) for s in range(PLE_TM // SUB)]
    chunks = [slice(n * PLE_TN, (n + 1) * PLE_TN) for n in range(D_MODEL // PLE_TN)]
    for rows in subs:
        h2 = h_ref[rows, :] + f_ref[rows, :]
        o_ref[rows, :] = h2
        e_scr[rows, :] = _rms_rows(h2, n3_ref[...]).astype(BF16)
        pb_scr[rows, :] = p_ref[rows, :].astype(BF16)
    for rows in subs:
        for cols in chunks:
            gate_logit = jnp.dot(e_scr[rows, :], wg_ref[:, cols], preferred_element_type=F32)
            emb = jnp.dot(pb_scr[rows, :], wp_ref[:, cols], preferred_element_type=F32)
            o_ref[rows, cols] += emb / (1.0 + jnp.exp(-gate_logit))
    for rows in subs:
        o_ref[rows, :] = _rms_rows(o_ref[rows, :], nf_ref[...])


def _ple(h1, f, p2, norm3_w, ple_gate_b, ple_proj_b, final_norm_w):
    m = h1.shape[0]
    ple_dim = p2.shape[1]
    blk = (2 * (3 * PLE_TM * D_MODEL * 4 + PLE_TM * ple_dim * 4) + D_MODEL * D_MODEL * 2 + ple_dim * D_MODEL * 2
           + PLE_TM * D_MODEL * 2 + PLE_TM * ple_dim * 2)
    row_spec = pl.BlockSpec((PLE_TM, D_MODEL), lambda i, j: (i, 0))
    return pl.pallas_call(
        _ple_kernel,
        grid=(m // PLE_TM, 1),
        in_specs=[
            row_spec,
            row_spec,
            pl.BlockSpec((PLE_TM, ple_dim), lambda i, j: (i, 0)),
            _resident((1, D_MODEL)),
            _resident((D_MODEL, D_MODEL)),
            _resident((ple_dim, D_MODEL)),
            _resident((1, D_MODEL)),
        ],
        out_specs=row_spec,
        out_shape=jax.ShapeDtypeStruct((m, D_MODEL), F32),
        scratch_shapes=[pltpu.VMEM((PLE_TM, D_MODEL), BF16), pltpu.VMEM((PLE_TM, ple_dim), BF16)],
        compiler_params=_params(blk),
        name="ple",
    )(h1, f, p2, norm3_w, ple_gate_b, ple_proj_b, final_norm_w)


def _gate_val(a, dtype):
    pad = [(0, 0)] * (a.ndim - 1) + [(0, D_FF_PAD - D_FF)]
    return (jnp.pad(a[..., :D_FF], pad).astype(dtype), jnp.pad(a[..., D_FF:], pad).astype(dtype))


def _stage_up_kernel(w_ref, g_ref, v_ref):
    zeros = jnp.zeros((STAGE_ROWS, D_FF_PAD - D_FF), BF16)
    g_ref[:, :D_FF] = w_ref[:, :D_FF].astype(BF16)
    g_ref[:, D_FF:] = zeros
    v_ref[:, :D_FF] = w_ref[:, D_FF:].astype(BF16)
    v_ref[:, D_FF:] = zeros


def _stage_up(w_up):
    k = w_up.shape[0]
    blk = 2 * (STAGE_ROWS * 2 * D_FF * 4 + 2 * STAGE_ROWS * D_FF_PAD * 2)
    out = jax.ShapeDtypeStruct((k, D_FF_PAD), BF16)
    return pl.pallas_call(
        _stage_up_kernel,
        grid=(k // STAGE_ROWS, 1),
        in_specs=[pl.BlockSpec((STAGE_ROWS, 2 * D_FF), lambda i, j: (i, 0))],
        out_specs=[pl.BlockSpec((STAGE_ROWS, D_FF_PAD), lambda i, j: (i, 0))] * 2,
        out_shape=[out, out],
        compiler_params=_params(blk),
        name="stage_up",
    )(w_up)


def _stage_down_kernel(w_ref, o_ref):
    row = pl.program_id(0) * FFN_TN + lax.broadcasted_iota(jnp.int32, (FFN_TN, D_MODEL), 0)
    o_ref[...] = jnp.where(row < D_FF, w_ref[...], 0.0).astype(BF16)


def _stage_down(w_down):
    blk = 2 * (FFN_TN * D_MODEL * 4 + FFN_TN * D_MODEL * 2)
    return pl.pallas_call(
        _stage_down_kernel,
        grid=(D_FF_PAD // FFN_TN, 1),
        in_specs=[pl.BlockSpec((FFN_TN, D_MODEL), lambda i, j: (i, 0))],
        out_specs=pl.BlockSpec((FFN_TN, D_MODEL), lambda i, j: (i, 0)),
        out_shape=jax.ShapeDtypeStruct((D_FF_PAD, D_MODEL), BF16),
        compiler_params=_params(blk),
        name="stage_down",
    )(w_down)


def kernel(x, p, norm1_w, w_in, pool_w, pool_scale, ret_gn_w, w_out, norm2_w, w_up, conv_w, conv_b, w_down,
           norm3_w, ple_gate_w, ple_proj_w, final_norm_w):
    batch, seq, d_model = x.shape
    assert w_in.shape[0] == 1, "one layer per call"
    assert d_model == D_MODEL and seq % (MIX_CHUNKS * RET_CHUNK) == 0 and seq % FFN_TM == 0
    m = batch * seq
    assert m % IN_TM == 0 and m % PLE_TM == 0
    row = lambda v: v.reshape(1, -1)

    w_in_b = w_in[0].astype(BF16)
    pool_w_b = pool_w[0].astype(BF16)
    w_out_b = w_out[0].astype(BF16)
    w_up_gv = _stage_up(w_up[0])
    conv_w_gv = _gate_val(conv_w[0], F32)
    conv_b_gv = _gate_val(row(conv_b[0]), F32)
    w_down_p = _stage_down(w_down[0])
    ple_gate_b = ple_gate_w[0].astype(BF16)
    ple_proj_b = ple_proj_w[0].astype(BF16)

    x2 = x.reshape(m, D_MODEL)
    z = _in_proj(x2, row(norm1_w[0]), w_in_b)
    h1, c = _mixer(z, pool_w_b, row(pool_scale[0]), row(ret_gn_w[0]), x2, w_out_b, row(norm2_w[0]), batch, seq)
    f = _ffn(c, w_up_gv, conv_w_gv, conv_b_gv, w_down_p, seq)
    out = _ple(h1, f, p[0].reshape(m, -1), row(norm3_w[0]), ple_gate_b, ple_proj_b, row(final_norm_w))
    return out.reshape(batch, seq, D_MODEL)
```

```python
import functools

import numpy as np
import jax
import jax.numpy as jnp
from jax import lax
from jax.experimental import pallas as pl
from jax.experimental.pallas import tpu as pltpu

D_MODEL = 2048
POOL_WIDTH = 1024
N_POOL_GROUPS = 4
POOL_GROUP_DIM = 256
POOL_WINDOWS = (2, 4, 8, 16)
RET_WIDTH = 1024
RET_HEADS = 8
RET_HEAD_DIM = 128
ROPE_BASE = 10000.0
IN_COLS = POOL_WIDTH + 4 * RET_WIDTH
D_FF = 5504
CONV_WIDTH = 3
NORM_EPS = 1e-6
GN_EPS = 1e-5

V7X_LANES = 128
V7X_SUBLANES = 8
V7X_VMEM_BYTES = 64 * 1024 * 1024

RET_CHUNK = 128
MIX_CHUNKS = 4
D_FF_PAD = 5632
FFN_TN = 512
FFN_TM = 1024
SUB = 256
IN_TM = 512
OUT_TN = 512
PLE_TM = 512
PLE_TN = 512

BF16 = jnp.bfloat16
F32 = jnp.float32


def _vmem_limit(block_bytes):
    return int(min(V7X_VMEM_BYTES - 6 * 1024 * 1024, block_bytes + 16 * 1024 * 1024))


def _params(block_bytes):
    return pltpu.CompilerParams(
        dimension_semantics=("arbitrary", "arbitrary"),
        vmem_limit_bytes=_vmem_limit(block_bytes),
    )


def _resident(shape):
    return pl.BlockSpec(shape, lambda i, j: (0,) * len(shape), pipeline_mode=pl.Buffered(1))


def _rms_rows(x, w):
    ms = jnp.mean(x * x, axis=-1, keepdims=True)
    return x * lax.rsqrt(ms + NORM_EPS) * w


def _in_proj_kernel(x_ref, nw_ref, w_ref, wup_ref, wdn_ref, wo_ref, pg_ref,
                    z_ref, upg_ref, upv_ref, dn_ref, wob_ref, pgb_ref, a_scr):
    n_sub = IN_TM // SUB
    for s in range(n_sub):
        rows = slice(s * SUB, (s + 1) * SUB)
        a_scr[rows, :] = _rms_rows(x_ref[rows, :], nw_ref[...]).astype(BF16)
    for s in range(n_sub):
        rows = slice(s * SUB, (s + 1) * SUB)
        z_ref[rows, :] = jnp.dot(a_scr[rows, :], w_ref[...], preferred_element_type=F32).astype(BF16)

    zeros = jnp.zeros((upg_ref.shape[0], D_FF_PAD - D_FF), BF16)
    upg_ref[:, :D_FF] = wup_ref[:, :D_FF].astype(BF16)
    upg_ref[:, D_FF:] = zeros
    upv_ref[:, :D_FF] = wup_ref[:, D_FF:].astype(BF16)
    upv_ref[:, D_FF:] = zeros
    dn_rows = dn_ref.shape[0]
    row = pl.program_id(0) * dn_rows + lax.broadcasted_iota(jnp.int32, dn_ref.shape, 0)
    dn_ref[...] = jnp.where(row < D_FF, wdn_ref[...], 0.0).astype(BF16)
    wob_ref[...] = wo_ref[...].astype(BF16)
    pgb_ref[...] = pg_ref[...].astype(BF16)


def _in_proj(x2, norm1_w, w_in_b, w_up, w_down, w_out, ple_gate_w):
    m = x2.shape[0]
    n = m // IN_TM
    k_rows = D_MODEL // n
    dn_rows = D_FF_PAD // n
    assert D_MODEL % n == 0 and D_FF_PAD % n == 0 and k_rows % 16 == 0 and dn_rows % 16 == 0
    blk = (2 * (IN_TM * D_MODEL * 4 + IN_TM * IN_COLS * 2) + D_MODEL * IN_COLS * 2 + IN_TM * D_MODEL * 2
           + 2 * (k_rows * 2 * D_FF * 4 + 2 * k_rows * D_FF_PAD * 2 + dn_rows * D_MODEL * 6 + 2 * k_rows * D_MODEL * 6))
    step_rows = lambda r, cols: pl.BlockSpec((r, cols), lambda i, j: (i, 0))
    return pl.pallas_call(
        _in_proj_kernel,
        grid=(n, 1),
        in_specs=[
            step_rows(IN_TM, D_MODEL),
            _resident((1, D_MODEL)),
            _resident((D_MODEL, IN_COLS)),
            step_rows(k_rows, 2 * D_FF),
            step_rows(dn_rows, D_MODEL),
            step_rows(k_rows, D_MODEL),
            step_rows(k_rows, D_MODEL),
        ],
        out_specs=[
            step_rows(IN_TM, IN_COLS),
            step_rows(k_rows, D_FF_PAD),
            step_rows(k_rows, D_FF_PAD),
            step_rows(dn_rows, D_MODEL),
            step_rows(k_rows, D_MODEL),
            step_rows(k_rows, D_MODEL),
        ],
        out_shape=[
            jax.ShapeDtypeStruct((m, IN_COLS), BF16),
            jax.ShapeDtypeStruct((D_MODEL, D_FF_PAD), BF16),
            jax.ShapeDtypeStruct((D_MODEL, D_FF_PAD), BF16),
            jax.ShapeDtypeStruct((D_FF_PAD, D_MODEL), BF16),
            jax.ShapeDtypeStruct((D_MODEL, D_MODEL), BF16),
            jax.ShapeDtypeStruct((D_MODEL, D_MODEL), BF16),
        ],
        scratch_shapes=[pltpu.VMEM((IN_TM, D_MODEL), BF16)],
        compiler_params=_params(blk),
        name="in_proj",
    )(x2, norm1_w, w_in_b, w_up, w_down, w_out, ple_gate_w)


def _mixer_tables(seq):
    c = RET_CHUNK
    pos = np.arange(seq, dtype=np.float64)
    inv_freq = 1.0 / (ROPE_BASE ** (np.arange(0, RET_HEAD_DIM, 2, dtype=np.float64) / RET_HEAD_DIM))
    ang = pos[:, None] * inv_freq[None, :]
    cos, sin = np.cos(ang), np.sin(ang)
    k_scale = RET_HEAD_DIM ** -0.5
    cc = np.concatenate([cos, cos], axis=1)
    ss = np.concatenate([-sin, sin], axis=1)
    rope = np.concatenate([cc, ss, cc * k_scale, ss * k_scale], axis=1)

    gamma = 1.0 - np.exp2(-5.0 - np.arange(RET_HEADS, dtype=np.float64))
    idx = np.arange(c, dtype=np.float64)
    diff = idx[:, None] - idx[None, :]
    decay = np.where(diff[None] >= 0, gamma[:, None, None] ** np.maximum(diff, 0.0)[None], 0.0)
    xi = gamma[None, :] ** (idx[:, None] + 1.0)
    zeta = gamma[None, :] ** (c - 1.0 - idx[:, None])
    xi_t = np.repeat(xi, RET_HEAD_DIM, axis=1)
    zeta_t = np.repeat(zeta, RET_HEAD_DIM, axis=1)
    g_chunk = tuple(float(g) for g in gamma ** c)

    band = np.zeros((N_POOL_GROUPS, c, 2 * c), dtype=np.float32)
    for g, w in enumerate(POOL_WINDOWS):
        for m_ in range(w):
            band[g, np.arange(c), c + np.arange(c) - m_] = 1.0
    t1 = np.concatenate([idx + 1.0, np.full(c, float(max(POOL_WINDOWS)))])
    inv_cnt = np.concatenate([np.repeat(1.0 / np.minimum(t1, float(w))[:, None], POOL_GROUP_DIM, axis=1)
                              for w in POOL_WINDOWS], axis=1)
    return (jnp.asarray(rope, F32), jnp.asarray(decay, F32), jnp.asarray(xi_t, F32),
            jnp.asarray(zeta_t, F32), jnp.asarray(band, BF16), jnp.asarray(inv_cnt, F32), g_chunk)


def _mixer_kernel(g_chunk, n_steps, blocks_per_seq, z_ref, rope_ref, decay_ref, xi_ref, zeta_ref, band_ref, inv_ref,
                  pw_ref, ps_ref, gn_ref, x_ref, wo_ref, n2_ref, h_ref, c_ref, uprev_scr, state_scr, mix_scr):
    c = RET_CHUNK
    hd = RET_HEAD_DIM
    g = pl.program_id(0)
    ci = lax.rem(jnp.minimum(g, n_steps - 1), blocks_per_seq)
    slot = lax.rem(g, 2)
    mix_cur = mix_scr.at[slot]
    mix_prev = mix_scr.at[1 - slot]
    heads = range(RET_HEADS)
    groups = range(N_POOL_GROUPS)
    nt = (((1,), (1,)), ((), ()))
    tn = (((0,), (0,)), ((), ()))

    @pl.when(g == 0)
    def _():
        mix_scr[...] = jnp.zeros_like(mix_scr)

    @pl.when(ci == 0)
    def _():
        uprev_scr[...] = jnp.zeros_like(uprev_scr)
        state_scr[...] = jnp.zeros_like(state_scr)

    n_col = D_MODEL // OUT_TN
    pieces = [(u, n) for u in range(MIX_CHUNKS * c // SUB) for n in range(n_col)]
    assert len(pieces) == 2 * MIX_CHUNKS

    def out_piece(k):
        u, n = pieces[k]
        rows = slice(u * SUB, (u + 1) * SUB)
        cols = slice(n * OUT_TN, (n + 1) * OUT_TN)
        acc = jnp.dot(mix_prev[rows, :], wo_ref[:, cols], preferred_element_type=F32)
        h_ref[rows, cols] = x_ref[rows, cols] + acc
        if n == n_col - 1:
            c_ref[rows, :] = _rms_rows(h_ref[rows, :], n2_ref[...]).astype(BF16)

    uprev_scr[c:, :] = z_ref[0:c, :POOL_WIDTH]
    st = [state_scr[h] for h in heads]

    for t in range(MIX_CHUNKS):
        rows = slice(t * c, (t + 1) * c)

        def zcols(part, h, rows=rows):
            lo = POOL_WIDTH + part * RET_WIDTH + h * hd
            return z_ref[rows, lo:lo + hd]

        qcc, qss = rope_ref[rows, 0:hd], rope_ref[rows, hd:2 * hd]
        kcc, kss = rope_ref[rows, 2 * hd:3 * hd], rope_ref[rows, 3 * hd:4 * hd]
        q_b, q_x, k_b, k_z = [], [], [], []
        for h in heads:
            hc = slice(h * hd, (h + 1) * hd)
            q = zcols(0, h).astype(F32)
            k = zcols(1, h).astype(F32)
            q_r = q * qcc + pltpu.roll(q, hd // 2, axis=1) * qss
            k_r = k * kcc + pltpu.roll(k, hd // 2, axis=1) * kss
            q_b.append(q_r.astype(BF16))
            q_x.append((q_r * xi_ref[:, hc]).astype(BF16))
            k_b.append(k_r.astype(BF16))
            k_z.append((k_r * zeta_ref[:, hc]).astype(BF16))

        def ucat(gp, t=t):
            cols = slice(gp * POOL_GROUP_DIM, (gp + 1) * POOL_GROUP_DIM)
            return uprev_scr[:, cols] if t == 0 else z_ref[(t - 1) * c:(t + 1) * c, cols]

        out_piece(2 * t if t == 0 else 2 * t - 1)
        sc = [lax.dot_general(q_b[h], k_b[h], nt, preferred_element_type=F32) for h in heads]
        kv = [lax.dot_general(k_z[h], zcols(2, h), tn, preferred_element_type=F32) for h in heads]
        wsum = [jnp.dot(band_ref[gp], ucat(gp), preferred_element_type=F32) for gp in groups]

        lhs = [jnp.concatenate([(sc[h] * decay_ref[h]).astype(BF16), q_x[h]], axis=1) for h in heads]
        rhs = [jnp.concatenate([zcols(2, h), st[h].astype(BF16)], axis=0) for h in heads]
        st = [g_chunk[h] * st[h] + kv[h] for h in heads]
        d_b = []
        for gp, w in enumerate(POOL_WINDOWS):
            cols = slice(gp * POOL_GROUP_DIM, (gp + 1) * POOL_GROUP_DIM)
            inv = inv_ref[:, cols] if t == 0 else 1.0 / w
            d_b.append((wsum[gp] * inv - z_ref[rows, cols].astype(F32)).astype(BF16))

        if t > 0:
            out_piece(2 * t)
        ret = [jnp.dot(lhs[h], rhs[h], preferred_element_type=F32) for h in heads]
        y = [jnp.dot(d_b[gp], pw_ref[gp], preferred_element_type=F32) for gp in groups]

        for gp in groups:
            cols = slice(gp * POOL_GROUP_DIM, (gp + 1) * POOL_GROUP_DIM)
            mix_cur[rows, cols] = (y[gp] * ps_ref[:, cols]).astype(BF16)
        for h in heads:
            hc = slice(h * hd, (h + 1) * hd)
            r = ret[h]
            mu = jnp.mean(r, axis=-1, keepdims=True)
            xc = r - mu
            var = jnp.mean(xc * xc, axis=-1, keepdims=True)
            rn = xc * lax.rsqrt(var + GN_EPS) * gn_ref[:, hc]
            gt = zcols(3, h).astype(F32)
            silu = gt / (1.0 + jnp.exp(-gt))
            mix_cur[rows, POOL_WIDTH + h * hd:POOL_WIDTH + (h + 1) * hd] = (silu * rn).astype(BF16)

    out_piece(2 * MIX_CHUNKS - 1)
    for h in heads:
        state_scr[h] = st[h]
    uprev_scr[:c, :] = z_ref[(MIX_CHUNKS - 1) * c:MIX_CHUNKS * c, :POOL_WIDTH]


def _mixer(z, pool_w_b, pool_scale, ret_gn_w, x2, w_out_b, norm2_w, batch, seq):
    c = RET_CHUNK
    rows = MIX_CHUNKS * c
    blocks_per_seq = seq // rows
    n_steps = batch * blocks_per_seq
    m = batch * seq
    rope, decay, xi_t, zeta_t, band, inv_cnt, g_chunk = _mixer_tables(seq)
    blk = (2 * (rows * IN_COLS * 2 + c * POOL_WIDTH * 4 + rows * 4 * RET_HEAD_DIM * 4
                + 2 * rows * D_MODEL * 4 + rows * D_MODEL * 2)
           + RET_HEADS * c * c * 4 + 2 * c * RET_WIDTH * 4 + N_POOL_GROUPS * c * 2 * c * 2
           + N_POOL_GROUPS * POOL_GROUP_DIM * POOL_GROUP_DIM * 2 + D_MODEL * D_MODEL * 2
           + 2 * c * POOL_WIDTH * 2 + RET_HEADS * RET_HEAD_DIM * RET_HEAD_DIM * 4 + 2 * rows * D_MODEL * 2)
    cur = lambda g: jnp.minimum(g, n_steps - 1)
    prev_spec = pl.BlockSpec((rows, D_MODEL), lambda g, j: (jnp.maximum(g - 1, 0), 0))
    return pl.pallas_call(
        functools.partial(_mixer_kernel, g_chunk, n_steps, blocks_per_seq),
        grid=(n_steps + 1, 1),
        in_specs=[
            pl.BlockSpec((rows, IN_COLS), lambda g, j: (cur(g), 0)),
            pl.BlockSpec((rows, 4 * RET_HEAD_DIM), lambda g, j: (cur(g) % blocks_per_seq, 0)),
            _resident((RET_HEADS, c, c)),
            _resident((c, RET_WIDTH)),
            _resident((c, RET_WIDTH)),
            _resident((N_POOL_GROUPS, c, 2 * c)),
            pl.BlockSpec((c, POOL_WIDTH), lambda g, j: (jnp.minimum(cur(g) % blocks_per_seq, 1), 0)),
            _resident((N_POOL_GROUPS, POOL_GROUP_DIM, POOL_GROUP_DIM)),
            _resident((1, POOL_WIDTH)),
            _resident((1, RET_WIDTH)),
            prev_spec,
            _resident((D_MODEL, D_MODEL)),
            _resident((1, D_MODEL)),
        ],
        out_specs=[prev_spec, prev_spec],
        out_shape=[jax.ShapeDtypeStruct((m, D_MODEL), F32), jax.ShapeDtypeStruct((m, D_MODEL), BF16)],
        scratch_shapes=[
            pltpu.VMEM((2 * c, POOL_WIDTH), BF16),
            pltpu.VMEM((RET_HEADS, RET_HEAD_DIM, RET_HEAD_DIM), F32),
            pltpu.VMEM((2, rows, D_MODEL), BF16),
        ],
        compiler_params=_params(blk),
        name="mixer",
    )(z, rope, decay, xi_t, zeta_t, band, inv_cnt, pool_w_b, pool_scale, ret_gn_w, x2, w_out_b, norm2_w)


FFN_SLABS = FFN_TN // V7X_LANES


def _ffn_kernel(tiles_per_seq, c_ref, wg_ref, wv_ref, cwg_ref, cwv_ref, cbg_ref, cbv_ref, wd_ref,
                o_ref, hg_scr, hv_scr, ug_scr, uv_scr):
    i = pl.program_id(0)
    j = pl.program_id(1)
    hh = V7X_SUBLANES

    @pl.when(jnp.logical_and(i == 0, j == 0))
    def _():
        hg_scr[...] = jnp.zeros_like(hg_scr)
        hv_scr[...] = jnp.zeros_like(hv_scr)

    @pl.when(j == 0)
    def _():
        o_ref[...] = jnp.zeros_like(o_ref)

    ug_scr[:, 0:hh, :] = hg_scr[j]
    uv_scr[:, 0:hh, :] = hv_scr[j]

    def up_store(s):
        c_b = c_ref[s * SUB:(s + 1) * SUB, :]
        base = hh + s * SUB
        ug = jnp.dot(c_b, wg_ref[...], preferred_element_type=F32)
        uv = jnp.dot(c_b, wv_ref[...], preferred_element_type=F32)
        for ct in range(FFN_SLABS):
            lanes = slice(ct * V7X_LANES, (ct + 1) * V7X_LANES)
            ug_scr[ct, base:base + SUB, :] = ug[:, lanes]
            uv_scr[ct, base:base + SUB, :] = uv[:, lanes]

    def conv(u_scr, s, cw, cb):
        base = hh + s * SUB
        outs = []
        for ct in range(FFN_SLABS):
            lanes = slice(ct * V7X_LANES, (ct + 1) * V7X_LANES)
            out = cb[:, lanes] + cw[CONV_WIDTH - 1:CONV_WIDTH, lanes] * u_scr[ct, base:base + SUB, :]
            for k in range(1, CONV_WIDTH):
                out = out + cw[CONV_WIDTH - 1 - k:CONV_WIDTH - k, lanes] * u_scr[ct, base - k:base - k + SUB, :]
            outs.append(out)
        return jnp.concatenate(outs, axis=1)

    n_sub = FFN_TM // SUB
    up_store(0)
    for s in range(n_sub):
        if s + 1 < n_sub:
            up_store(s + 1)
        gate = conv(ug_scr, s, cwg_ref[...], cbg_ref[...])
        val = conv(uv_scr, s, cwv_ref[...], cbv_ref[...])
        act = (gate / (1.0 + jnp.exp(-gate)) * val).astype(BF16)
        o_ref[s * SUB:(s + 1) * SUB, :] += jnp.dot(act, wd_ref[...], preferred_element_type=F32)
    keep = (i + 1) % tiles_per_seq != 0
    hg_scr[j] = jnp.where(keep, ug_scr[:, FFN_TM:FFN_TM + hh, :], 0.0)
    hv_scr[j] = jnp.where(keep, uv_scr[:, FFN_TM:FFN_TM + hh, :], 0.0)


def _ffn(c, w_up_gv, conv_w_gv, conv_b_gv, w_down_p, seq):
    m = c.shape[0]
    nj = D_FF_PAD // FFN_TN
    halo = (nj, FFN_SLABS, V7X_SUBLANES, V7X_LANES)
    slab = (FFN_SLABS, V7X_SUBLANES + FFN_TM, V7X_LANES)
    blk = (2 * (FFN_TM * D_MODEL * 2 + FFN_TM * D_MODEL * 4 + 2 * D_MODEL * FFN_TN * 2 + FFN_TN * D_MODEL * 2)
           + 2 * int(np.prod(halo)) * 4 + 2 * int(np.prod(slab)) * 4)
    return pl.pallas_call(
        functools.partial(_ffn_kernel, seq // FFN_TM),
        grid=(m // FFN_TM, nj),
        in_specs=[
            pl.BlockSpec((FFN_TM, D_MODEL), lambda i, j: (i, 0)),
            pl.BlockSpec((D_MODEL, FFN_TN), lambda i, j: (0, j)),
            pl.BlockSpec((D_MODEL, FFN_TN), lambda i, j: (0, j)),
            pl.BlockSpec((CONV_WIDTH, FFN_TN), lambda i, j: (0, j)),
            pl.BlockSpec((CONV_WIDTH, FFN_TN), lambda i, j: (0, j)),
            pl.BlockSpec((1, FFN_TN), lambda i, j: (0, j)),
            pl.BlockSpec((1, FFN_TN), lambda i, j: (0, j)),
            pl.BlockSpec((FFN_TN, D_MODEL), lambda i, j: (j, 0)),
        ],
        out_specs=pl.BlockSpec((FFN_TM, D_MODEL), lambda i, j: (i, 0)),
        out_shape=jax.ShapeDtypeStruct((m, D_MODEL), F32),
        scratch_shapes=[pltpu.VMEM(halo, F32), pltpu.VMEM(halo, F32), pltpu.VMEM(slab, F32), pltpu.VMEM(slab, F32)],
        compiler_params=_params(blk),
        name="ffn",
    )(c, *w_up_gv, *conv_w_gv, *conv_b_gv, w_down_p)


def _ple_kernel(h_ref, f_ref, p_ref, n3_ref, wg_ref, wp_ref, nf_ref, o_ref, e_scr, pb_scr):
    subs = [slice(s * SUB, (s + 1) * SUB) for s in range(PLE_TM // SUB)]
    chunks = [slice(n * PLE_TN, (n + 1) * PLE_TN) for n in range(D_MODEL // PLE_TN)]
    for rows in subs:
        h2 = h_ref[rows, :] + f_ref[rows, :]
        o_ref[rows, :] = h2
        e_scr[rows, :] = _rms_rows(h2, n3_ref[...]).astype(BF16)
        pb_scr[rows, :] = p_ref[rows, :].astype(BF16)
    for rows in subs:
        for cols in chunks:
            gate_logit = jnp.dot(e_scr[rows, :], wg_ref[:, cols], preferred_element_type=F32)
            emb = jnp.dot(pb_scr[rows, :], wp_ref[:, cols], preferred_element_type=F32)
            o_ref[rows, cols] += emb / (1.0 + jnp.exp(-gate_logit))
    for rows in subs:
        o_ref[rows, :] = _rms_rows(o_ref[rows, :], nf_ref[...])


def _ple(h1, f, p2, norm3_w, ple_gate_b, ple_proj_b, final_norm_w):
    m = h1.shape[0]
    ple_dim = p2.shape[1]
    blk = (2 * (3 * PLE_TM * D_MODEL * 4 + PLE_TM * ple_dim * 4) + D_MODEL * D_MODEL * 2 + ple_dim * D_MODEL * 2
           + PLE_TM * D_MODEL * 2 + PLE_TM * ple_dim * 2)
    row_spec = pl.BlockSpec((PLE_TM, D_MODEL), lambda i, j: (i, 0))
    return pl.pallas_call(
        _ple_kernel,
        grid=(m // PLE_TM, 1),
        in_specs=[
            row_spec,
            row_spec,
            pl.BlockSpec((PLE_TM, ple_dim), lambda i, j: (i, 0)),
            _resident((1, D_MODEL)),
            _resident((D_MODEL, D_MODEL)),
            _resident((ple_dim, D_MODEL)),
            _resident((1, D_MODEL)),
        ],
        out_specs=row_spec,
        out_shape=jax.ShapeDtypeStruct((m, D_MODEL), F32),
        scratch_shapes=[pltpu.VMEM((PLE_TM, D_MODEL), BF16), pltpu.VMEM((PLE_TM, ple_dim), BF16)],
        compiler_params=_params(blk),
        name="ple",
    )(h1, f, p2, norm3_w, ple_gate_b, ple_proj_b, final_norm_w)


def _gate_val(a, dtype):
    pad = [(0, 0)] * (a.ndim - 1) + [(0, D_FF_PAD - D_FF)]
    return (jnp.pad(a[..., :D_FF], pad).astype(dtype), jnp.pad(a[..., D_FF:], pad).astype(dtype))


def kernel(x, p, norm1_w, w_in, pool_w, pool_scale, ret_gn_w, w_out, norm2_w, w_up, conv_w, conv_b, w_down,
           norm3_w, ple_gate_w, ple_proj_w, final_norm_w):
    batch, seq, d_model = x.shape
    assert w_in.shape[0] == 1, "one layer per call"
    assert d_model == D_MODEL and seq % (MIX_CHUNKS * RET_CHUNK) == 0 and seq % FFN_TM == 0
    m = batch * seq
    assert m % IN_TM == 0 and m % PLE_TM == 0
    row = lambda v: v.reshape(1, -1)

    w_in_b = w_in[0].astype(BF16)
    pool_w_b = pool_w[0].astype(BF16)
    conv_w_gv = _gate_val(conv_w[0], F32)
    conv_b_gv = _gate_val(row(conv_b[0]), F32)
    ple_proj_b = ple_proj_w[0].astype(BF16)

    x2 = x.reshape(m, D_MODEL)
    z, w_up_g, w_up_v, w_down_p, w_out_b, ple_gate_b = _in_proj(
        x2, row(norm1_w[0]), w_in_b, w_up[0], w_down[0], w_out[0], ple_gate_w[0])
    h1, c = _mixer(z, pool_w_b, row(pool_scale[0]), row(ret_gn_w[0]), x2, w_out_b, row(norm2_w[0]), batch, seq)
    f = _ffn(c, (w_up_g, w_up_v), conv_w_gv, conv_b_gv, w_down_p, seq)
    out = _ple(h1, f, p[0].reshape(m, -1), row(norm3_w[0]), ple_gate_b, ple_proj_b, row(final_norm_w))
    return out.reshape(batch, seq, D_MODEL)
```

```python
import functools

import numpy as np
import jax
import jax.numpy as jnp
from jax import lax
from jax.experimental import pallas as pl
from jax.experimental.pallas import tpu as pltpu

D_MODEL = 2048
POOL_WIDTH = 1024
N_POOL_GROUPS = 4
POOL_GROUP_DIM = 256
POOL_WINDOWS = (2, 4, 8, 16)
RET_WIDTH = 1024
RET_HEADS = 8
RET_HEAD_DIM = 128
ROPE_BASE = 10000.0
IN_COLS = POOL_WIDTH + 4 * RET_WIDTH
D_FF = 5504
CONV_WIDTH = 3
NORM_EPS = 1e-6
GN_EPS = 1e-5

V7X_LANES = 128
V7X_SUBLANES = 8
V7X_BF16_SUBLANES = 16
V7X_VMEM_BYTES = 64 * 1024 * 1024
V7X_VMEM_UNSCOPED_BYTES = 6 * 1024 * 1024
TEMPORARIES_BYTES = 16 * 1024 * 1024

RET_CHUNK = 128
MIX_CHUNKS = 4
D_FF_PAD = 5632
FFN_TN = 512
FFN_TM = 1024
SUB = 256
IN_TM = 512
OUT_TN = 512
PLE_TM = 512
PLE_TN = 512

BF16 = jnp.bfloat16
F32 = jnp.float32


def _vmem_limit(block_bytes):
    return int(min(V7X_VMEM_BYTES - V7X_VMEM_UNSCOPED_BYTES, block_bytes + TEMPORARIES_BYTES))


def _params(block_bytes):
    return pltpu.CompilerParams(
        dimension_semantics=("arbitrary", "arbitrary"),
        vmem_limit_bytes=_vmem_limit(block_bytes),
    )


def _resident(shape):
    return pl.BlockSpec(shape, lambda i, j: (0,) * len(shape), pipeline_mode=pl.Buffered(1))


def _rms_rows(x, w):
    ms = jnp.mean(x * x, axis=-1, keepdims=True)
    return x * lax.rsqrt(ms + NORM_EPS) * w


def _in_proj_kernel(x_ref, nw_ref, w_ref, wup_ref, wdn_ref, wo_ref, pg_ref,
                    z_ref, upg_ref, upv_ref, dn_ref, wob_ref, pgb_ref, a_scr):
    n_sub = IN_TM // SUB
    for s in range(n_sub):
        rows = slice(s * SUB, (s + 1) * SUB)
        a_scr[rows, :] = _rms_rows(x_ref[rows, :], nw_ref[...]).astype(BF16)
    for s in range(n_sub):
        rows = slice(s * SUB, (s + 1) * SUB)
        z_ref[rows, :] = jnp.dot(a_scr[rows, :], w_ref[...], preferred_element_type=F32).astype(BF16)

    zeros = jnp.zeros((upg_ref.shape[0], D_FF_PAD - D_FF), BF16)
    upg_ref[:, :D_FF] = wup_ref[:, :D_FF].astype(BF16)
    upg_ref[:, D_FF:] = zeros
    upv_ref[:, :D_FF] = wup_ref[:, D_FF:].astype(BF16)
    upv_ref[:, D_FF:] = zeros
    dn_rows = dn_ref.shape[0]
    row = pl.program_id(0) * dn_rows + lax.broadcasted_iota(jnp.int32, dn_ref.shape, 0)
    dn_ref[...] = jnp.where(row < D_FF, wdn_ref[...], 0.0).astype(BF16)
    wob_ref[...] = wo_ref[...].astype(BF16)
    pgb_ref[...] = pg_ref[...].astype(BF16)


def _in_proj(x2, norm1_w, w_in_b, w_up, w_down, w_out, ple_gate_w):
    m = x2.shape[0]
    n = m // IN_TM
    k_rows = D_MODEL // n
    dn_rows = D_FF_PAD // n
    assert D_MODEL % n == 0 and D_FF_PAD % n == 0
    assert k_rows % V7X_BF16_SUBLANES == 0 and dn_rows % V7X_BF16_SUBLANES == 0
    blk = (2 * (IN_TM * D_MODEL * 4 + IN_TM * IN_COLS * 2) + D_MODEL * IN_COLS * 2 + IN_TM * D_MODEL * 2
           + 2 * (k_rows * 2 * D_FF * 4 + 2 * k_rows * D_FF_PAD * 2 + dn_rows * D_MODEL * 6 + 2 * k_rows * D_MODEL * 6))
    step_rows = lambda r, cols: pl.BlockSpec((r, cols), lambda i, j: (i, 0))
    return pl.pallas_call(
        _in_proj_kernel,
        grid=(n, 1),
        in_specs=[
            step_rows(IN_TM, D_MODEL),
            _resident((1, D_MODEL)),
            _resident((D_MODEL, IN_COLS)),
            step_rows(k_rows, 2 * D_FF),
            step_rows(dn_rows, D_MODEL),
            step_rows(k_rows, D_MODEL),
            step_rows(k_rows, D_MODEL),
        ],
        out_specs=[
            step_rows(IN_TM, IN_COLS),
            step_rows(k_rows, D_FF_PAD),
            step_rows(k_rows, D_FF_PAD),
            step_rows(dn_rows, D_MODEL),
            step_rows(k_rows, D_MODEL),
            step_rows(k_rows, D_MODEL),
        ],
        out_shape=[
            jax.ShapeDtypeStruct((m, IN_COLS), BF16),
            jax.ShapeDtypeStruct((D_MODEL, D_FF_PAD), BF16),
            jax.ShapeDtypeStruct((D_MODEL, D_FF_PAD), BF16),
            jax.ShapeDtypeStruct((D_FF_PAD, D_MODEL), BF16),
            jax.ShapeDtypeStruct((D_MODEL, D_MODEL), BF16),
            jax.ShapeDtypeStruct((D_MODEL, D_MODEL), BF16),
        ],
        scratch_shapes=[pltpu.VMEM((IN_TM, D_MODEL), BF16)],
        compiler_params=_params(blk),
        name="in_proj",
    )(x2, norm1_w, w_in_b, w_up, w_down, w_out, ple_gate_w)


def _mixer_tables(seq):
    c = RET_CHUNK
    pos = np.arange(seq, dtype=np.float64)
    inv_freq = 1.0 / (ROPE_BASE ** (np.arange(0, RET_HEAD_DIM, 2, dtype=np.float64) / RET_HEAD_DIM))
    ang = pos[:, None] * inv_freq[None, :]
    cos, sin = np.cos(ang), np.sin(ang)
    k_scale = RET_HEAD_DIM ** -0.5
    cc = np.concatenate([cos, cos], axis=1)
    ss = np.concatenate([-sin, sin], axis=1)
    rope = np.concatenate([cc, ss, cc * k_scale, ss * k_scale], axis=1)

    gamma = 1.0 - np.exp2(-5.0 - np.arange(RET_HEADS, dtype=np.float64))
    idx = np.arange(c, dtype=np.float64)
    diff = idx[:, None] - idx[None, :]
    decay = np.where(diff[None] >= 0, gamma[:, None, None] ** np.maximum(diff, 0.0)[None], 0.0)
    xi = gamma[None, :] ** (idx[:, None] + 1.0)
    zeta = gamma[None, :] ** (c - 1.0 - idx[:, None])
    xi_t = np.repeat(xi, RET_HEAD_DIM, axis=1)
    zeta_t = np.repeat(zeta, RET_HEAD_DIM, axis=1)
    g_chunk = tuple(float(g) for g in gamma ** c)

    band = np.zeros((N_POOL_GROUPS, c, 2 * c), dtype=np.float32)
    for g, w in enumerate(POOL_WINDOWS):
        for m_ in range(w):
            band[g, np.arange(c), c + np.arange(c) - m_] = 1.0
    t1 = np.concatenate([idx + 1.0, np.full(c, float(max(POOL_WINDOWS)))])
    inv_cnt = np.concatenate([np.repeat(1.0 / np.minimum(t1, float(w))[:, None], POOL_GROUP_DIM, axis=1)
                              for w in POOL_WINDOWS], axis=1)
    return (jnp.asarray(rope, F32), jnp.asarray(decay, F32), jnp.asarray(xi_t, F32),
            jnp.asarray(zeta_t, F32), jnp.asarray(band, BF16), jnp.asarray(inv_cnt, F32), g_chunk)


def _mixer_kernel(g_chunk, n_steps, blocks_per_seq, z_ref, rope_ref, decay_ref, xi_ref, zeta_ref, band_ref, inv_ref,
                  pw_ref, ps_ref, gn_ref, x_ref, wo_ref, n2_ref, h_ref, c_ref, uprev_scr, state_scr, mix_scr):
    c = RET_CHUNK
    hd = RET_HEAD_DIM
    g = pl.program_id(0)
    ci = lax.rem(jnp.minimum(g, n_steps - 1), blocks_per_seq)
    slot = lax.rem(g, 2)
    mix_cur = mix_scr.at[slot]
    mix_prev = mix_scr.at[1 - slot]
    heads = range(RET_HEADS)
    groups = range(N_POOL_GROUPS)
    nt = (((1,), (1,)), ((), ()))
    tn = (((0,), (0,)), ((), ()))

    @pl.when(g == 0)
    def _():
        mix_scr[...] = jnp.zeros_like(mix_scr)

    @pl.when(ci == 0)
    def _():
        uprev_scr[...] = jnp.zeros_like(uprev_scr)
        state_scr[...] = jnp.zeros_like(state_scr)

    n_col = D_MODEL // OUT_TN
    pieces = [(u, n) for u in range(MIX_CHUNKS * c // SUB) for n in range(n_col)]
    assert len(pieces) == 2 * MIX_CHUNKS

    def out_piece(k):
        u, n = pieces[k]
        rows = slice(u * SUB, (u + 1) * SUB)
        cols = slice(n * OUT_TN, (n + 1) * OUT_TN)
        acc = jnp.dot(mix_prev[rows, :], wo_ref[:, cols], preferred_element_type=F32)
        h_ref[rows, cols] = x_ref[rows, cols] + acc
        if n == n_col - 1:
            c_ref[rows, :] = _rms_rows(h_ref[rows, :], n2_ref[...]).astype(BF16)

    uprev_scr[c:, :] = z_ref[0:c, :POOL_WIDTH]
    st = [state_scr[h] for h in heads]

    for t in range(MIX_CHUNKS):
        rows = slice(t * c, (t + 1) * c)

        def zcols(part, h, rows=rows):
            lo = POOL_WIDTH + part * RET_WIDTH + h * hd
            return z_ref[rows, lo:lo + hd]

        qcc, qss = rope_ref[rows, 0:hd], rope_ref[rows, hd:2 * hd]
        kcc, kss = rope_ref[rows, 2 * hd:3 * hd], rope_ref[rows, 3 * hd:4 * hd]
        q_b, q_x, k_b, k_z = [], [], [], []
        for h in heads:
            hc = slice(h * hd, (h + 1) * hd)
            q = zcols(0, h).astype(F32)
            k = zcols(1, h).astype(F32)
            q_r = q * qcc + pltpu.roll(q, hd // 2, axis=1) * qss
            k_r = k * kcc + pltpu.roll(k, hd // 2, axis=1) * kss
            q_b.append(q_r.astype(BF16))
            q_x.append((q_r * xi_ref[:, hc]).astype(BF16))
            k_b.append(k_r.astype(BF16))
            k_z.append((k_r * zeta_ref[:, hc]).astype(BF16))

        def ucat(gp, t=t):
            cols = slice(gp * POOL_GROUP_DIM, (gp + 1) * POOL_GROUP_DIM)
            return uprev_scr[:, cols] if t == 0 else z_ref[(t - 1) * c:(t + 1) * c, cols]

        out_piece(2 * t if t == 0 else 2 * t - 1)
        sc = [lax.dot_general(q_b[h], k_b[h], nt, preferred_element_type=F32) for h in heads]
        kv = [lax.dot_general(k_z[h], zcols(2, h), tn, preferred_element_type=F32) for h in heads]
        wsum = [jnp.dot(band_ref[gp], ucat(gp), preferred_element_type=F32) for gp in groups]

        lhs = [jnp.concatenate([(sc[h] * decay_ref[h]).astype(BF16), q_x[h]], axis=1) for h in heads]
        rhs = [jnp.concatenate([zcols(2, h), st[h].astype(BF16)], axis=0) for h in heads]
        st = [g_chunk[h] * st[h] + kv[h] for h in heads]
        d_b = []
        for gp, w in enumerate(POOL_WINDOWS):
            cols = slice(gp * POOL_GROUP_DIM, (gp + 1) * POOL_GROUP_DIM)
            inv = inv_ref[:, cols] if t == 0 else 1.0 / w
            d_b.append((wsum[gp] * inv - z_ref[rows, cols].astype(F32)).astype(BF16))

        if t > 0:
            out_piece(2 * t)
        ret = [jnp.dot(lhs[h], rhs[h], preferred_element_type=F32) for h in heads]
        y = [jnp.dot(d_b[gp], pw_ref[gp], preferred_element_type=F32) for gp in groups]

        for gp in groups:
            cols = slice(gp * POOL_GROUP_DIM, (gp + 1) * POOL_GROUP_DIM)
            mix_cur[rows, cols] = (y[gp] * ps_ref[:, cols]).astype(BF16)
        for h in heads:
            hc = slice(h * hd, (h + 1) * hd)
            r = ret[h]
            mu = jnp.mean(r, axis=-1, keepdims=True)
            xc = r - mu
            var = jnp.mean(xc * xc, axis=-1, keepdims=True)
            rn = xc * lax.rsqrt(var + GN_EPS) * gn_ref[:, hc]
            gt = zcols(3, h).astype(F32)
            silu = gt / (1.0 + jnp.exp(-gt))
            mix_cur[rows, POOL_WIDTH + h * hd:POOL_WIDTH + (h + 1) * hd] = (silu * rn).astype(BF16)

    out_piece(2 * MIX_CHUNKS - 1)
    for h in heads:
        state_scr[h] = st[h]
    uprev_scr[:c, :] = z_ref[(MIX_CHUNKS - 1) * c:MIX_CHUNKS * c, :POOL_WIDTH]


def _mixer(z, pool_w_b, pool_scale, ret_gn_w, x2, w_out_b, norm2_w, batch, seq):
    c = RET_CHUNK
    rows = MIX_CHUNKS * c
    blocks_per_seq = seq // rows
    n_steps = batch * blocks_per_seq
    m = batch * seq
    rope, decay, xi_t, zeta_t, band, inv_cnt, g_chunk = _mixer_tables(seq)
    blk = (2 * (rows * IN_COLS * 2 + c * POOL_WIDTH * 4 + rows * 4 * RET_HEAD_DIM * 4
                + 2 * rows * D_MODEL * 4 + rows * D_MODEL * 2)
           + RET_HEADS * c * c * 4 + 2 * c * RET_WIDTH * 4 + N_POOL_GROUPS * c * 2 * c * 2
           + N_POOL_GROUPS * POOL_GROUP_DIM * POOL_GROUP_DIM * 2 + D_MODEL * D_MODEL * 2
           + 2 * c * POOL_WIDTH * 2 + RET_HEADS * RET_HEAD_DIM * RET_HEAD_DIM * 4 + 2 * rows * D_MODEL * 2)
    cur = lambda g: jnp.minimum(g, n_steps - 1)
    prev_spec = pl.BlockSpec((rows, D_MODEL), lambda g, j: (jnp.maximum(g - 1, 0), 0))
    return pl.pallas_call(
        functools.partial(_mixer_kernel, g_chunk, n_steps, blocks_per_seq),
        grid=(n_steps + 1, 1),
        in_specs=[
            pl.BlockSpec((rows, IN_COLS), lambda g, j: (cur(g), 0)),
            pl.BlockSpec((rows, 4 * RET_HEAD_DIM), lambda g, j: (cur(g) % blocks_per_seq, 0)),
            _resident((RET_HEADS, c, c)),
            _resident((c, RET_WIDTH)),
            _resident((c, RET_WIDTH)),
            _resident((N_POOL_GROUPS, c, 2 * c)),
            pl.BlockSpec((c, POOL_WIDTH), lambda g, j: (jnp.minimum(cur(g) % blocks_per_seq, 1), 0)),
            _resident((N_POOL_GROUPS, POOL_GROUP_DIM, POOL_GROUP_DIM)),
            _resident((1, POOL_WIDTH)),
            _resident((1, RET_WIDTH)),
            prev_spec,
            _resident((D_MODEL, D_MODEL)),
            _resident((1, D_MODEL)),
        ],
        out_specs=[prev_spec, prev_spec],
        out_shape=[jax.ShapeDtypeStruct((m, D_MODEL), F32), jax.ShapeDtypeStruct((m, D_MODEL), BF16)],
        scratch_shapes=[
            pltpu.VMEM((2 * c, POOL_WIDTH), BF16),
            pltpu.VMEM((RET_HEADS, RET_HEAD_DIM, RET_HEAD_DIM), F32),
            pltpu.VMEM((2, rows, D_MODEL), BF16),
        ],
        compiler_params=_params(blk),
        name="mixer",
    )(z, rope, decay, xi_t, zeta_t, band, inv_cnt, pool_w_b, pool_scale, ret_gn_w, x2, w_out_b, norm2_w)


FFN_SLABS = FFN_TN // V7X_LANES


def _ffn_kernel(tiles_per_seq, c_ref, wg_ref, wv_ref, cwg_ref, cwv_ref, cbg_ref, cbv_ref, wd_ref,
                o_ref, hg_scr, hv_scr, ug_scr, uv_scr):
    i = pl.program_id(0)
    j = pl.program_id(1)
    hh = V7X_SUBLANES

    @pl.when(jnp.logical_and(i == 0, j == 0))
    def _():
        hg_scr[...] = jnp.zeros_like(hg_scr)
        hv_scr[...] = jnp.zeros_like(hv_scr)

    @pl.when(j == 0)
    def _():
        o_ref[...] = jnp.zeros_like(o_ref)

    ug_scr[:, 0:hh, :] = hg_scr[j]
    uv_scr[:, 0:hh, :] = hv_scr[j]

    c_b = c_ref[...]
    ug = jnp.dot(c_b, wg_ref[...], preferred_element_type=F32)
    uv = jnp.dot(c_b, wv_ref[...], preferred_element_type=F32)
    for ct in range(FFN_SLABS):
        lanes = slice(ct * V7X_LANES, (ct + 1) * V7X_LANES)
        ug_scr[ct, hh:hh + FFN_TM, :] = ug[:, lanes]
        uv_scr[ct, hh:hh + FFN_TM, :] = uv[:, lanes]

    def conv(u_scr, cw, cb):
        outs = []
        for ct in range(FFN_SLABS):
            lanes = slice(ct * V7X_LANES, (ct + 1) * V7X_LANES)
            out = cb[:, lanes] + cw[CONV_WIDTH - 1:CONV_WIDTH, lanes] * u_scr[ct, hh:hh + FFN_TM, :]
            for k in range(1, CONV_WIDTH):
                out = out + cw[CONV_WIDTH - 1 - k:CONV_WIDTH - k, lanes] * u_scr[ct, hh - k:hh - k + FFN_TM, :]
            outs.append(out)
        return jnp.concatenate(outs, axis=1)

    gate = conv(ug_scr, cwg_ref[...], cbg_ref[...])
    val = conv(uv_scr, cwv_ref[...], cbv_ref[...])
    act = (gate / (1.0 + jnp.exp(-gate)) * val).astype(BF16)
    o_ref[...] += jnp.dot(act, wd_ref[...], preferred_element_type=F32)
    keep = (i + 1) % tiles_per_seq != 0
    hg_scr[j] = jnp.where(keep, ug_scr[:, FFN_TM:FFN_TM + hh, :], 0.0)
    hv_scr[j] = jnp.where(keep, uv_scr[:, FFN_TM:FFN_TM + hh, :], 0.0)


def _ffn(c, w_up_gv, conv_w_gv, conv_b_gv, w_down_p, seq):
    m = c.shape[0]
    nj = D_FF_PAD // FFN_TN
    halo = (nj, FFN_SLABS, V7X_SUBLANES, V7X_LANES)
    slab = (FFN_SLABS, V7X_SUBLANES + FFN_TM, V7X_LANES)
    blk = (2 * (FFN_TM * D_MODEL * 2 + FFN_TM * D_MODEL * 4 + 2 * D_MODEL * FFN_TN * 2 + FFN_TN * D_MODEL * 2)
           + 2 * int(np.prod(halo)) * 4 + 2 * int(np.prod(slab)) * 4)
    return pl.pallas_call(
        functools.partial(_ffn_kernel, seq // FFN_TM),
        grid=(m // FFN_TM, nj),
        in_specs=[
            pl.BlockSpec((FFN_TM, D_MODEL), lambda i, j: (i, 0)),
            pl.BlockSpec((D_MODEL, FFN_TN), lambda i, j: (0, j)),
            pl.BlockSpec((D_MODEL, FFN_TN), lambda i, j: (0, j)),
            pl.BlockSpec((CONV_WIDTH, FFN_TN), lambda i, j: (0, j)),
            pl.BlockSpec((CONV_WIDTH, FFN_TN), lambda i, j: (0, j)),
            pl.BlockSpec((1, FFN_TN), lambda i, j: (0, j)),
            pl.BlockSpec((1, FFN_TN), lambda i, j: (0, j)),
            pl.BlockSpec((FFN_TN, D_MODEL), lambda i, j: (j, 0)),
        ],
        out_specs=pl.BlockSpec((FFN_TM, D_MODEL), lambda i, j: (i, 0)),
        out_shape=jax.ShapeDtypeStruct((m, D_MODEL), F32),
        scratch_shapes=[pltpu.VMEM(halo, F32), pltpu.VMEM(halo, F32), pltpu.VMEM(slab, F32), pltpu.VMEM(slab, F32)],
        compiler_params=_params(blk),
        name="ffn",
    )(c, *w_up_gv, *conv_w_gv, *conv_b_gv, w_down_p)


def _ple_kernel(h_ref, f_ref, p_ref, n3_ref, wg_ref, wp_ref, nf_ref, o_ref, e_scr, pb_scr):
    subs = [slice(s * SUB, (s + 1) * SUB) for s in range(PLE_TM // SUB)]
    chunks = [slice(n * PLE_TN, (n + 1) * PLE_TN) for n in range(D_MODEL // PLE_TN)]
    for rows in subs:
        h2 = h_ref[rows, :] + f_ref[rows, :]
        o_ref[rows, :] = h2
        e_scr[rows, :] = _rms_rows(h2, n3_ref[...]).astype(BF16)
        pb_scr[rows, :] = p_ref[rows, :].astype(BF16)
    for rows in subs:
        for cols in chunks:
            gate_logit = jnp.dot(e_scr[rows, :], wg_ref[:, cols], preferred_element_type=F32)
            emb = jnp.dot(pb_scr[rows, :], wp_ref[:, cols], preferred_element_type=F32)
            o_ref[rows, cols] += emb / (1.0 + jnp.exp(-gate_logit))
    for rows in subs:
        o_ref[rows, :] = _rms_rows(o_ref[rows, :], nf_ref[...])


def _ple(h1, f, p2, norm3_w, ple_gate_b, ple_proj_b, final_norm_w):
    m = h1.shape[0]
    ple_dim = p2.shape[1]
    blk = (2 * (3 * PLE_TM * D_MODEL * 4 + PLE_TM * ple_dim * 4) + D_MODEL * D_MODEL * 2 + ple_dim * D_MODEL * 2
           + PLE_TM * D_MODEL * 2 + PLE_TM * ple_dim * 2)
    row_spec = pl.BlockSpec((PLE_TM, D_MODEL), lambda i, j: (i, 0))
    return pl.pallas_call(
        _ple_kernel,
        grid=(m // PLE_TM, 1),
        in_specs=[
            row_spec,
            row_spec,
            pl.BlockSpec((PLE_TM, ple_dim), lambda i, j: (i, 0)),
            _resident((1, D_MODEL)),
            _resident((D_MODEL, D_MODEL)),
            _resident((ple_dim, D_MODEL)),
            _resident((1, D_MODEL)),
        ],
        out_specs=row_spec,
        out_shape=jax.ShapeDtypeStruct((m, D_MODEL), F32),
        scratch_shapes=[pltpu.VMEM((PLE_TM, D_MODEL), BF16), pltpu.VMEM((PLE_TM, ple_dim), BF16)],
        compiler_params=_params(blk),
        name="ple",
    )(h1, f, p2, norm3_w, ple_gate_b, ple_proj_b, final_norm_w)


def _gate_val(a, dtype):
    pad = [(0, 0)] * (a.ndim - 1) + [(0, D_FF_PAD - D_FF)]
    return (jnp.pad(a[..., :D_FF], pad).astype(dtype), jnp.pad(a[..., D_FF:], pad).astype(dtype))


def kernel(x, p, norm1_w, w_in, pool_w, pool_scale, ret_gn_w, w_out, norm2_w, w_up, conv_w, conv_b, w_down,
           norm3_w, ple_gate_w, ple_proj_w, final_norm_w):
    batch, seq, d_model = x.shape
    assert w_in.shape[0] == 1, "one layer per call"
    assert d_model == D_MODEL and seq % (MIX_CHUNKS * RET_CHUNK) == 0 and seq % FFN_TM == 0
    m = batch * seq
    assert m % IN_TM == 0 and m % PLE_TM == 0
    row = lambda v: v.reshape(1, -1)

    w_in_b = w_in[0].astype(BF16)
    pool_w_b = pool_w[0].astype(BF16)
    conv_w_gv = _gate_val(conv_w[0], F32)
    conv_b_gv = _gate_val(row(conv_b[0]), F32)
    ple_proj_b = ple_proj_w[0].astype(BF16)

    x2 = x.reshape(m, D_MODEL)
    z, w_up_g, w_up_v, w_down_p, w_out_b, ple_gate_b = _in_proj(
        x2, row(norm1_w[0]), w_in_b, w_up[0], w_down[0], w_out[0], ple_gate_w[0])
    h1, c = _mixer(z, pool_w_b, row(pool_scale[0]), row(ret_gn_w[0]), x2, w_out_b, row(norm2_w[0]), batch, seq)
    f = _ffn(c, (w_up_g, w_up_v), conv_w_gv, conv_b_gv, w_down_p, seq)
    out = _ple(h1, f, p[0].reshape(m, -1), row(norm3_w[0]), ple_gate_b, ple_proj_b, row(final_norm_w))
    return out.reshape(batch, seq, D_MODEL)
```

```python
import functools

import numpy as np
import jax
import jax.numpy as jnp
from jax import lax
from jax.experimental import pallas as pl
from jax.experimental.pallas import tpu as pltpu

D_MODEL = 2048
POOL_WIDTH = 1024
N_POOL_GROUPS = 4
POOL_GROUP_DIM = 256
POOL_WINDOWS = (2, 4, 8, 16)
RET_WIDTH = 1024
RET_HEADS = 8
RET_HEAD_DIM = 128
ROPE_BASE = 10000.0
IN_COLS = POOL_WIDTH + 4 * RET_WIDTH
Z_Q, Z_K, Z_V, Z_G = (POOL_WIDTH + i * RET_WIDTH for i in range(4))
D_FF = 5504
CONV_WIDTH = 3
NORM_EPS = 1e-6
GN_EPS = 1e-5

V7X_LANES = 128
V7X_SUBLANES = 8
V7X_BF16_SUBLANES = 16
V7X_VMEM_BYTES = 64 * 1024 * 1024
V7X_VMEM_UNSCOPED_BYTES = 6 * 1024 * 1024
TEMPORARIES_BYTES = 16 * 1024 * 1024

RET_CHUNK = 128
MIX_CHUNKS = 4
D_FF_PAD = 5632
FFN_TN = 512
FFN_TM = 1024
SUB = 256
IN_TM = 512
OUT_TN = 512
PLE_TM = 512
PLE_TN = 512

BF16 = jnp.bfloat16
F32 = jnp.float32


def _vmem_limit(block_bytes):
    return int(min(V7X_VMEM_BYTES - V7X_VMEM_UNSCOPED_BYTES, block_bytes + TEMPORARIES_BYTES))


def _params(block_bytes):
    return pltpu.CompilerParams(
        dimension_semantics=("arbitrary", "arbitrary"),
        vmem_limit_bytes=_vmem_limit(block_bytes),
    )


def _resident(shape):
    return pl.BlockSpec(shape, lambda i, j: (0,) * len(shape), pipeline_mode=pl.Buffered(1))


def _rms_rows(x, w):
    ms = jnp.mean(x * x, axis=-1, keepdims=True)
    return x * lax.rsqrt(ms + NORM_EPS) * w


def _in_proj_kernel(x_ref, nw_ref, w_ref, rope_ref, wup_ref, wdn_ref, wo_ref, pg_ref,
                    z_ref, upg_ref, upv_ref, dn_ref, wob_ref, pgb_ref, a_scr):
    hd = RET_HEAD_DIM
    n_sub = IN_TM // SUB
    for s in range(n_sub):
        rows = slice(s * SUB, (s + 1) * SUB)
        a_scr[rows, :] = _rms_rows(x_ref[rows, :], nw_ref[...]).astype(BF16)

    def proj(rows, lo):
        return jnp.dot(a_scr[rows, :], w_ref[:, lo:lo + RET_WIDTH], preferred_element_type=F32)

    def rotary(t, cc, ss):
        parts = []
        for h in range(RET_HEADS):
            th = t[:, h * hd:(h + 1) * hd]
            parts.append(th * cc + pltpu.roll(th, hd // 2, axis=1) * ss)
        return jnp.concatenate(parts, axis=1)

    for s in range(n_sub):
        rows = slice(s * SUB, (s + 1) * SUB)
        z_ref[rows, 0:POOL_WIDTH] = proj(rows, 0).astype(BF16)
        z_ref[rows, Z_Q:Z_Q + RET_WIDTH] = rotary(proj(rows, Z_Q), rope_ref[rows, 0:hd],
                                                  rope_ref[rows, hd:2 * hd]).astype(BF16)
        z_ref[rows, Z_K:Z_K + RET_WIDTH] = rotary(proj(rows, Z_K), rope_ref[rows, 2 * hd:3 * hd],
                                                  rope_ref[rows, 3 * hd:4 * hd]).astype(BF16)
        z_ref[rows, Z_V:Z_V + RET_WIDTH] = proj(rows, Z_V).astype(BF16)
        gt = proj(rows, Z_G)
        z_ref[rows, Z_G:Z_G + RET_WIDTH] = (gt / (1.0 + jnp.exp(-gt))).astype(BF16)

    zeros = jnp.zeros((upg_ref.shape[0], D_FF_PAD - D_FF), BF16)
    upg_ref[:, :D_FF] = wup_ref[:, :D_FF].astype(BF16)
    upg_ref[:, D_FF:] = zeros
    upv_ref[:, :D_FF] = wup_ref[:, D_FF:].astype(BF16)
    upv_ref[:, D_FF:] = zeros
    dn_rows = dn_ref.shape[0]
    row = pl.program_id(0) * dn_rows + lax.broadcasted_iota(jnp.int32, dn_ref.shape, 0)
    dn_ref[...] = jnp.where(row < D_FF, wdn_ref[...], 0.0).astype(BF16)
    wob_ref[...] = wo_ref[...].astype(BF16)
    pgb_ref[...] = pg_ref[...].astype(BF16)


def _in_proj(x2, norm1_w, w_in_b, rope, w_up, w_down, w_out, ple_gate_w, seq):
    m = x2.shape[0]
    n = m // IN_TM
    tiles_per_seq = seq // IN_TM
    k_rows = D_MODEL // n
    dn_rows = D_FF_PAD // n
    assert D_MODEL % n == 0 and D_FF_PAD % n == 0
    assert k_rows % V7X_BF16_SUBLANES == 0 and dn_rows % V7X_BF16_SUBLANES == 0
    blk = (2 * (IN_TM * D_MODEL * 4 + IN_TM * IN_COLS * 2 + IN_TM * 4 * RET_HEAD_DIM * 4) + D_MODEL * IN_COLS * 2
           + IN_TM * D_MODEL * 2
           + 2 * (k_rows * 2 * D_FF * 4 + 2 * k_rows * D_FF_PAD * 2 + dn_rows * D_MODEL * 6 + 2 * k_rows * D_MODEL * 6))
    step_rows = lambda r, cols: pl.BlockSpec((r, cols), lambda i, j: (i, 0))
    return pl.pallas_call(
        _in_proj_kernel,
        grid=(n, 1),
        in_specs=[
            step_rows(IN_TM, D_MODEL),
            _resident((1, D_MODEL)),
            _resident((D_MODEL, IN_COLS)),
            pl.BlockSpec((IN_TM, 4 * RET_HEAD_DIM), lambda i, j: (i % tiles_per_seq, 0)),
            step_rows(k_rows, 2 * D_FF),
            step_rows(dn_rows, D_MODEL),
            step_rows(k_rows, D_MODEL),
            step_rows(k_rows, D_MODEL),
        ],
        out_specs=[
            step_rows(IN_TM, IN_COLS),
            step_rows(k_rows, D_FF_PAD),
            step_rows(k_rows, D_FF_PAD),
            step_rows(dn_rows, D_MODEL),
            step_rows(k_rows, D_MODEL),
            step_rows(k_rows, D_MODEL),
        ],
        out_shape=[
            jax.ShapeDtypeStruct((m, IN_COLS), BF16),
            jax.ShapeDtypeStruct((D_MODEL, D_FF_PAD), BF16),
            jax.ShapeDtypeStruct((D_MODEL, D_FF_PAD), BF16),
            jax.ShapeDtypeStruct((D_FF_PAD, D_MODEL), BF16),
            jax.ShapeDtypeStruct((D_MODEL, D_MODEL), BF16),
            jax.ShapeDtypeStruct((D_MODEL, D_MODEL), BF16),
        ],
        scratch_shapes=[pltpu.VMEM((IN_TM, D_MODEL), BF16)],
        compiler_params=_params(blk),
        name="in_proj",
    )(x2, norm1_w, w_in_b, rope, w_up, w_down, w_out, ple_gate_w)


def _mixer_tables(seq):
    c = RET_CHUNK
    pos = np.arange(seq, dtype=np.float64)
    inv_freq = 1.0 / (ROPE_BASE ** (np.arange(0, RET_HEAD_DIM, 2, dtype=np.float64) / RET_HEAD_DIM))
    ang = pos[:, None] * inv_freq[None, :]
    cos, sin = np.cos(ang), np.sin(ang)
    k_scale = RET_HEAD_DIM ** -0.5
    cc = np.concatenate([cos, cos], axis=1)
    ss = np.concatenate([-sin, sin], axis=1)
    rope = np.concatenate([cc, ss, cc * k_scale, ss * k_scale], axis=1)

    gamma = 1.0 - np.exp2(-5.0 - np.arange(RET_HEADS, dtype=np.float64))
    idx = np.arange(c, dtype=np.float64)
    diff = idx[:, None] - idx[None, :]
    decay = np.where(diff[None] >= 0, gamma[:, None, None] ** np.maximum(diff, 0.0)[None], 0.0)
    xi = gamma[None, :] ** (idx[:, None] + 1.0)
    zeta = gamma[None, :] ** (c - 1.0 - idx[:, None])
    xi_t = np.repeat(xi, RET_HEAD_DIM, axis=1)
    zeta_t = np.repeat(zeta, RET_HEAD_DIM, axis=1)
    g_chunk = tuple(float(g) for g in gamma ** c)

    band = np.zeros((N_POOL_GROUPS, c, 2 * c), dtype=np.float32)
    for g, w in enumerate(POOL_WINDOWS):
        for m_ in range(w):
            band[g, np.arange(c), c + np.arange(c) - m_] = 1.0
    t1 = np.concatenate([idx + 1.0, np.full(c, float(max(POOL_WINDOWS)))])
    inv_cnt = np.concatenate([np.repeat(1.0 / np.minimum(t1, float(w))[:, None], POOL_GROUP_DIM, axis=1)
                              for w in POOL_WINDOWS], axis=1)
    return (jnp.asarray(rope, F32), jnp.asarray(decay, F32), jnp.asarray(xi_t, F32),
            jnp.asarray(zeta_t, F32), jnp.asarray(band, BF16), jnp.asarray(inv_cnt, F32), g_chunk)


def _mixer_kernel(g_chunk, n_steps, blocks_per_seq, z_ref, decay_ref, xi_ref, zeta_ref, band_ref, inv_ref,
                  pw_ref, ps_ref, gn_ref, x_ref, wo_ref, n2_ref, h_ref, c_ref, uprev_scr, state_scr, mix_scr):
    c = RET_CHUNK
    hd = RET_HEAD_DIM
    g = pl.program_id(0)
    ci = lax.rem(jnp.minimum(g, n_steps - 1), blocks_per_seq)
    mix_cur = mix_scr.at[0]
    mix_prev = mix_scr.at[1]
    heads = range(RET_HEADS)
    groups = range(N_POOL_GROUPS)
    nt = (((1,), (1,)), ((), ()))
    tn = (((0,), (0,)), ((), ()))

    @pl.when(g == 0)
    def _():
        mix_scr[...] = jnp.zeros_like(mix_scr)

    @pl.when(ci == 0)
    def _():
        uprev_scr[...] = jnp.zeros_like(uprev_scr)
        state_scr[...] = jnp.zeros_like(state_scr)

    n_col = D_MODEL // OUT_TN
    pieces = [(u, n) for u in range(MIX_CHUNKS * c // SUB) for n in range(n_col)]
    assert len(pieces) == 2 * MIX_CHUNKS

    def out_piece(k):
        u, n = pieces[k]
        rows = slice(u * SUB, (u + 1) * SUB)
        cols = slice(n * OUT_TN, (n + 1) * OUT_TN)
        acc = jnp.dot(mix_prev[rows, :], wo_ref[:, cols], preferred_element_type=F32)
        h_ref[rows, cols] = x_ref[rows, cols] + acc
        if n == n_col - 1:
            c_ref[rows, :] = _rms_rows(h_ref[rows, :], n2_ref[...]).astype(BF16)

    uprev_scr[c:, :] = z_ref[0:c, :POOL_WIDTH]
    st = [state_scr[h] for h in heads]

    for t in range(MIX_CHUNKS):
        rows = slice(t * c, (t + 1) * c)

        def zcols(part, h, rows=rows):
            lo = POOL_WIDTH + part * RET_WIDTH + h * hd
            return z_ref[rows, lo:lo + hd]

        q_b = [zcols(0, h) for h in heads]
        k_b = [zcols(1, h) for h in heads]
        q_x = [(q_b[h].astype(F32) * xi_ref[:, h * hd:(h + 1) * hd]).astype(BF16) for h in heads]
        k_z = [(k_b[h].astype(F32) * zeta_ref[:, h * hd:(h + 1) * hd]).astype(BF16) for h in heads]

        def ucat(gp, t=t):
            cols = slice(gp * POOL_GROUP_DIM, (gp + 1) * POOL_GROUP_DIM)
            return uprev_scr[:, cols] if t == 0 else z_ref[(t - 1) * c:(t + 1) * c, cols]

        out_piece(2 * t)
        sc = [lax.dot_general(q_b[h], k_b[h], nt, preferred_element_type=F32) for h in heads]
        kv = [lax.dot_general(k_z[h], zcols(2, h), tn, preferred_element_type=F32) for h in heads]
        wsum = [jnp.dot(band_ref[gp], ucat(gp), preferred_element_type=F32) for gp in groups]

        lhs = [jnp.concatenate([(sc[h] * decay_ref[h]).astype(BF16), q_x[h]], axis=1) for h in heads]
        rhs = [jnp.concatenate([zcols(2, h), st[h].astype(BF16)], axis=0) for h in heads]
        st = [g_chunk[h] * st[h] + kv[h] for h in heads]
        d_b = []
        for gp, w in enumerate(POOL_WINDOWS):
            cols = slice(gp * POOL_GROUP_DIM, (gp + 1) * POOL_GROUP_DIM)
            inv = inv_ref[:, cols] if t == 0 else 1.0 / w
            d_b.append((wsum[gp] * inv - z_ref[rows, cols].astype(F32)).astype(BF16))

        out_piece(2 * t + 1)
        ret = [jnp.dot(lhs[h], rhs[h], preferred_element_type=F32) for h in heads]
        y = [jnp.dot(d_b[gp], pw_ref[gp], preferred_element_type=F32) for gp in groups]

        for gp in groups:
            cols = slice(gp * POOL_GROUP_DIM, (gp + 1) * POOL_GROUP_DIM)
            mix_cur[rows, cols] = (y[gp] * ps_ref[:, cols]).astype(BF16)
        for h in heads:
            hc = slice(h * hd, (h + 1) * hd)
            r = ret[h]
            mu = jnp.mean(r, axis=-1, keepdims=True)
            xc = r - mu
            var = jnp.mean(xc * xc, axis=-1, keepdims=True)
            rn = xc * lax.rsqrt(var + GN_EPS) * gn_ref[:, hc]
            silu = zcols(3, h).astype(F32)
            mix_cur[rows, POOL_WIDTH + h * hd:POOL_WIDTH + (h + 1) * hd] = (silu * rn).astype(BF16)

    for h in heads:
        state_scr[h] = st[h]
    uprev_scr[:c, :] = z_ref[(MIX_CHUNKS - 1) * c:MIX_CHUNKS * c, :POOL_WIDTH]
    mix_prev[...] = mix_cur[...]


def _mixer(z, tables, pool_w_b, pool_scale, ret_gn_w, x2, w_out_b, norm2_w, batch, seq):
    c = RET_CHUNK
    rows = MIX_CHUNKS * c
    blocks_per_seq = seq // rows
    n_steps = batch * blocks_per_seq
    m = batch * seq
    decay, xi_t, zeta_t, band, inv_cnt, g_chunk = tables
    blk = (2 * (rows * IN_COLS * 2 + c * POOL_WIDTH * 4
                + 2 * rows * D_MODEL * 4 + rows * D_MODEL * 2)
           + RET_HEADS * c * c * 4 + 2 * c * RET_WIDTH * 4 + N_POOL_GROUPS * c * 2 * c * 2
           + N_POOL_GROUPS * POOL_GROUP_DIM * POOL_GROUP_DIM * 2 + D_MODEL * D_MODEL * 2
           + 2 * c * POOL_WIDTH * 2 + RET_HEADS * RET_HEAD_DIM * RET_HEAD_DIM * 4 + 2 * rows * D_MODEL * 2)
    cur = lambda g: jnp.minimum(g, n_steps - 1)
    prev_spec = pl.BlockSpec((rows, D_MODEL), lambda g, j: (jnp.maximum(g - 1, 0), 0))
    return pl.pallas_call(
        functools.partial(_mixer_kernel, g_chunk, n_steps, blocks_per_seq),
        grid=(n_steps + 1, 1),
        in_specs=[
            pl.BlockSpec((rows, IN_COLS), lambda g, j: (cur(g), 0)),
            _resident((RET_HEADS, c, c)),
            _resident((c, RET_WIDTH)),
            _resident((c, RET_WIDTH)),
            _resident((N_POOL_GROUPS, c, 2 * c)),
            pl.BlockSpec((c, POOL_WIDTH), lambda g, j: (jnp.minimum(cur(g) % blocks_per_seq, 1), 0)),
            _resident((N_POOL_GROUPS, POOL_GROUP_DIM, POOL_GROUP_DIM)),
            _resident((1, POOL_WIDTH)),
            _resident((1, RET_WIDTH)),
            prev_spec,
            _resident((D_MODEL, D_MODEL)),
            _resident((1, D_MODEL)),
        ],
        out_specs=[prev_spec, prev_spec],
        out_shape=[jax.ShapeDtypeStruct((m, D_MODEL), F32), jax.ShapeDtypeStruct((m, D_MODEL), BF16)],
        scratch_shapes=[
            pltpu.VMEM((2 * c, POOL_WIDTH), BF16),
            pltpu.VMEM((RET_HEADS, RET_HEAD_DIM, RET_HEAD_DIM), F32),
            pltpu.VMEM((2, rows, D_MODEL), BF16),
        ],
        compiler_params=_params(blk),
        name="mixer",
    )(z, decay, xi_t, zeta_t, band, inv_cnt, pool_w_b, pool_scale, ret_gn_w, x2, w_out_b, norm2_w)


FFN_SLABS = FFN_TN // V7X_LANES


def _ffn_kernel(tiles_per_seq, c_ref, wg_ref, wv_ref, cwg_ref, cwv_ref, cbg_ref, cbv_ref, wd_ref,
                o_ref, hg_scr, hv_scr, ug_scr, uv_scr):
    i = pl.program_id(0)
    j = pl.program_id(1)
    hh = V7X_SUBLANES

    @pl.when(jnp.logical_and(i == 0, j == 0))
    def _():
        hg_scr[...] = jnp.zeros_like(hg_scr)
        hv_scr[...] = jnp.zeros_like(hv_scr)

    @pl.when(j == 0)
    def _():
        o_ref[...] = jnp.zeros_like(o_ref)

    ug_scr[:, 0:hh, :] = hg_scr[j]
    uv_scr[:, 0:hh, :] = hv_scr[j]

    c_b = c_ref[...]
    ug = jnp.dot(c_b, wg_ref[...], preferred_element_type=F32)
    uv = jnp.dot(c_b, wv_ref[...], preferred_element_type=F32)
    for ct in range(FFN_SLABS):
        lanes = slice(ct * V7X_LANES, (ct + 1) * V7X_LANES)
        ug_scr[ct, hh:hh + FFN_TM, :] = ug[:, lanes]
        uv_scr[ct, hh:hh + FFN_TM, :] = uv[:, lanes]

    def conv(u_scr, cw, cb):
        outs = []
        for ct in range(FFN_SLABS):
            lanes = slice(ct * V7X_LANES, (ct + 1) * V7X_LANES)
            out = cb[:, lanes] + cw[CONV_WIDTH - 1:CONV_WIDTH, lanes] * u_scr[ct, hh:hh + FFN_TM, :]
            for k in range(1, CONV_WIDTH):
                out = out + cw[CONV_WIDTH - 1 - k:CONV_WIDTH - k, lanes] * u_scr[ct, hh - k:hh - k + FFN_TM, :]
            outs.append(out)
        return jnp.concatenate(outs, axis=1)

    gate = conv(ug_scr, cwg_ref[...], cbg_ref[...])
    val = conv(uv_scr, cwv_ref[...], cbv_ref[...])
    act = (gate / (1.0 + jnp.exp(-gate)) * val).astype(BF16)
    o_ref[...] += jnp.dot(act, wd_ref[...], preferred_element_type=F32)
    keep = (i + 1) % tiles_per_seq != 0
    hg_scr[j] = jnp.where(keep, ug_scr[:, FFN_TM:FFN_TM + hh, :], 0.0)
    hv_scr[j] = jnp.where(keep, uv_scr[:, FFN_TM:FFN_TM + hh, :], 0.0)


def _ffn(c, w_up_gv, conv_w_gv, conv_b_gv, w_down_p, seq):
    m = c.shape[0]
    nj = D_FF_PAD // FFN_TN
    halo = (nj, FFN_SLABS, V7X_SUBLANES, V7X_LANES)
    slab = (FFN_SLABS, V7X_SUBLANES + FFN_TM, V7X_LANES)
    blk = (2 * (FFN_TM * D_MODEL * 2 + FFN_TM * D_MODEL * 4 + 2 * D_MODEL * FFN_TN * 2 + FFN_TN * D_MODEL * 2)
           + 2 * int(np.prod(halo)) * 4 + 2 * int(np.prod(slab)) * 4)
    return pl.pallas_call(
        functools.partial(_ffn_kernel, seq // FFN_TM),
        grid=(m // FFN_TM, nj),
        in_specs=[
            pl.BlockSpec((FFN_TM, D_MODEL), lambda i, j: (i, 0)),
            pl.BlockSpec((D_MODEL, FFN_TN), lambda i, j: (0, j)),
            pl.BlockSpec((D_MODEL, FFN_TN), lambda i, j: (0, j)),
            pl.BlockSpec((CONV_WIDTH, FFN_TN), lambda i, j: (0, j)),
            pl.BlockSpec((CONV_WIDTH, FFN_TN), lambda i, j: (0, j)),
            pl.BlockSpec((1, FFN_TN), lambda i, j: (0, j)),
            pl.BlockSpec((1, FFN_TN), lambda i, j: (0, j)),
            pl.BlockSpec((FFN_TN, D_MODEL), lambda i, j: (j, 0)),
        ],
        out_specs=pl.BlockSpec((FFN_TM, D_MODEL), lambda i, j: (i, 0)),
        out_shape=jax.ShapeDtypeStruct((m, D_MODEL), F32),
        scratch_shapes=[pltpu.VMEM(halo, F32), pltpu.VMEM(halo, F32), pltpu.VMEM(slab, F32), pltpu.VMEM(slab, F32)],
        compiler_params=_params(blk),
        name="ffn",
    )(c, *w_up_gv, *conv_w_gv, *conv_b_gv, w_down_p)


def _ple_kernel(h_ref, f_ref, p_ref, n3_ref, wg_ref, wp_ref, nf_ref, o_ref, e_scr, pb_scr):
    subs = [slice(s * SUB, (s + 1) * SUB) for s in range(PLE_TM // SUB)]
    chunks = [slice(n * PLE_TN, (n + 1) * PLE_TN) for n in range(D_MODEL // PLE_TN)]
    for rows in subs:
        h2 = h_ref[rows, :] + f_ref[rows, :]
        o_ref[rows, :] = h2
        e_scr[rows, :] = _rms_rows(h2, n3_ref[...]).astype(BF16)
        pb_scr[rows, :] = p_ref[rows, :].astype(BF16)
    for rows in subs:
        for cols in chunks:
            gate_logit = jnp.dot(e_scr[rows, :], wg_ref[:, cols], preferred_element_type=F32)
            emb = jnp.dot(pb_scr[rows, :], wp_ref[:, cols], preferred_element_type=F32)
            o_ref[rows, cols] += emb / (1.0 + jnp.exp(-gate_logit))
    for rows in subs:
        o_ref[rows, :] = _rms_rows(o_ref[rows, :], nf_ref[...])


def _ple(h1, f, p2, norm3_w, ple_gate_b, ple_proj_b, final_norm_w):
    m = h1.shape[0]
    ple_dim = p2.shape[1]
    blk = (2 * (3 * PLE_TM * D_MODEL * 4 + PLE_TM * ple_dim * 4) + D_MODEL * D_MODEL * 2 + ple_dim * D_MODEL * 2
           + PLE_TM * D_MODEL * 2 + PLE_TM * ple_dim * 2)
    row_spec = pl.BlockSpec((PLE_TM, D_MODEL), lambda i, j: (i, 0))
    return pl.pallas_call(
        _ple_kernel,
        grid=(m // PLE_TM, 1),
        in_specs=[
            row_spec,
            row_spec,
            pl.BlockSpec((PLE_TM, ple_dim), lambda i, j: (i, 0)),
            _resident((1, D_MODEL)),
            _resident((D_MODEL, D_MODEL)),
            _resident((ple_dim, D_MODEL)),
            _resident((1, D_MODEL)),
        ],
        out_specs=row_spec,
        out_shape=jax.ShapeDtypeStruct((m, D_MODEL), F32),
        scratch_shapes=[pltpu.VMEM((PLE_TM, D_MODEL), BF16), pltpu.VMEM((PLE_TM, ple_dim), BF16)],
        compiler_params=_params(blk),
        name="ple",
    )(h1, f, p2, norm3_w, ple_gate_b, ple_proj_b, final_norm_w)


def _gate_val(a, dtype):
    pad = [(0, 0)] * (a.ndim - 1) + [(0, D_FF_PAD - D_FF)]
    return (jnp.pad(a[..., :D_FF], pad).astype(dtype), jnp.pad(a[..., D_FF:], pad).astype(dtype))


def kernel(x, p, norm1_w, w_in, pool_w, pool_scale, ret_gn_w, w_out, norm2_w, w_up, conv_w, conv_b, w_down,
           norm3_w, ple_gate_w, ple_proj_w, final_norm_w):
    batch, seq, d_model = x.shape
    assert w_in.shape[0] == 1, "one layer per call"
    assert d_model == D_MODEL and seq % (MIX_CHUNKS * RET_CHUNK) == 0 and seq % FFN_TM == 0
    m = batch * seq
    assert m % IN_TM == 0 and seq % IN_TM == 0 and m % PLE_TM == 0
    row = lambda v: v.reshape(1, -1)

    w_in_b = w_in[0].astype(BF16)
    pool_w_b = pool_w[0].astype(BF16)
    conv_w_gv = _gate_val(conv_w[0], F32)
    conv_b_gv = _gate_val(row(conv_b[0]), F32)
    ple_proj_b = ple_proj_w[0].astype(BF16)

    x2 = x.reshape(m, D_MODEL)
    rope, *tables = _mixer_tables(seq)
    z, w_up_g, w_up_v, w_down_p, w_out_b, ple_gate_b = _in_proj(
        x2, row(norm1_w[0]), w_in_b, rope, w_up[0], w_down[0], w_out[0], ple_gate_w[0], seq)
    h1, c = _mixer(z, tables, pool_w_b, row(pool_scale[0]), row(ret_gn_w[0]), x2, w_out_b, row(norm2_w[0]), batch, seq)
    f = _ffn(c, (w_up_g, w_up_v), conv_w_gv, conv_b_gv, w_down_p, seq)
    out = _ple(h1, f, p[0].reshape(m, -1), row(norm3_w[0]), ple_gate_b, ple_proj_b, row(final_norm_w))
    return out.reshape(batch, seq, D_MODEL)
```

```python
import functools

import numpy as np
import jax
import jax.numpy as jnp
from jax import lax
from jax.experimental import pallas as pl
from jax.experimental.pallas import tpu as pltpu

D_MODEL = 2048
POOL_WIDTH = 1024
N_POOL_GROUPS = 4
POOL_GROUP_DIM = 256
POOL_WINDOWS = (2, 4, 8, 16)
RET_WIDTH = 1024
RET_HEADS = 8
RET_HEAD_DIM = 128
ROPE_BASE = 10000.0
IN_COLS = POOL_WIDTH + 4 * RET_WIDTH
Z_Q, Z_K, Z_V, Z_G = (POOL_WIDTH + i * RET_WIDTH for i in range(4))
D_FF = 5504
CONV_WIDTH = 3
NORM_EPS = 1e-6
GN_EPS = 1e-5

V7X_LANES = 128
V7X_SUBLANES = 8
V7X_BF16_SUBLANES = 16
V7X_VMEM_BYTES = 64 * 1024 * 1024
V7X_VMEM_UNSCOPED_BYTES = 6 * 1024 * 1024
TEMPORARIES_BYTES = 16 * 1024 * 1024

RET_CHUNK = 128
MIX_CHUNKS = 4
D_FF_PAD = 5632
FFN_TN = 512
FFN_TM = 1024
SUB = 256
IN_TM = 512
OUT_TN = 512
PLE_TM = 512
PLE_TN = 512

BF16 = jnp.bfloat16
F32 = jnp.float32


def _vmem_limit(block_bytes):
    return int(min(V7X_VMEM_BYTES - V7X_VMEM_UNSCOPED_BYTES, block_bytes + TEMPORARIES_BYTES))


def _params(block_bytes):
    return pltpu.CompilerParams(
        dimension_semantics=("arbitrary", "arbitrary"),
        vmem_limit_bytes=_vmem_limit(block_bytes),
    )


def _resident(shape):
    return pl.BlockSpec(shape, lambda i, j: (0,) * len(shape), pipeline_mode=pl.Buffered(1))


def _rms_rows(x, w):
    ms = jnp.mean(x * x, axis=-1, keepdims=True)
    return x * lax.rsqrt(ms + NORM_EPS) * w


def _in_proj_kernel(x_ref, nw_ref, w_ref, rope_ref, wup_ref, wdn_ref, wo_ref, pg_ref,
                    z_ref, upg_ref, upv_ref, dn_ref, wob_ref, pgb_ref, a_scr):
    hd = RET_HEAD_DIM
    n_sub = IN_TM // SUB
    for s in range(n_sub):
        rows = slice(s * SUB, (s + 1) * SUB)
        a_scr[rows, :] = _rms_rows(x_ref[rows, :], nw_ref[...]).astype(BF16)

    def proj(rows, lo):
        return jnp.dot(a_scr[rows, :], w_ref[:, lo:lo + RET_WIDTH], preferred_element_type=F32)

    def rotary(t, cc, ss):
        parts = []
        for h in range(RET_HEADS):
            th = t[:, h * hd:(h + 1) * hd]
            parts.append(th * cc + pltpu.roll(th, hd // 2, axis=1) * ss)
        return jnp.concatenate(parts, axis=1)

    for s in range(n_sub):
        rows = slice(s * SUB, (s + 1) * SUB)
        gt = proj(rows, Z_G)
        z_ref[rows, Z_G:Z_G + RET_WIDTH] = (gt / (1.0 + jnp.exp(-gt))).astype(BF16)
        z_ref[rows, Z_Q:Z_Q + RET_WIDTH] = rotary(proj(rows, Z_Q), rope_ref[rows, 0:hd],
                                                  rope_ref[rows, hd:2 * hd]).astype(BF16)
        z_ref[rows, Z_K:Z_K + RET_WIDTH] = rotary(proj(rows, Z_K), rope_ref[rows, 2 * hd:3 * hd],
                                                  rope_ref[rows, 3 * hd:4 * hd]).astype(BF16)
        z_ref[rows, 0:POOL_WIDTH] = proj(rows, 0).astype(BF16)
        z_ref[rows, Z_V:Z_V + RET_WIDTH] = proj(rows, Z_V).astype(BF16)

    zeros = jnp.zeros((upg_ref.shape[0], D_FF_PAD - D_FF), BF16)
    upg_ref[:, :D_FF] = wup_ref[:, :D_FF].astype(BF16)
    upg_ref[:, D_FF:] = zeros
    upv_ref[:, :D_FF] = wup_ref[:, D_FF:].astype(BF16)
    upv_ref[:, D_FF:] = zeros
    dn_rows = dn_ref.shape[0]
    row = pl.program_id(0) * dn_rows + lax.broadcasted_iota(jnp.int32, dn_ref.shape, 0)
    dn_ref[...] = jnp.where(row < D_FF, wdn_ref[...], 0.0).astype(BF16)
    wob_ref[...] = wo_ref[...].astype(BF16)
    pgb_ref[...] = pg_ref[...].astype(BF16)


def _in_proj(x2, norm1_w, w_in_b, rope, w_up, w_down, w_out, ple_gate_w, seq):
    m = x2.shape[0]
    n = m // IN_TM
    tiles_per_seq = seq // IN_TM
    k_rows = D_MODEL // n
    dn_rows = D_FF_PAD // n
    assert D_MODEL % n == 0 and D_FF_PAD % n == 0
    assert k_rows % V7X_BF16_SUBLANES == 0 and dn_rows % V7X_BF16_SUBLANES == 0
    blk = (2 * (IN_TM * D_MODEL * 4 + IN_TM * IN_COLS * 2 + IN_TM * 4 * RET_HEAD_DIM * 4) + D_MODEL * IN_COLS * 2
           + IN_TM * D_MODEL * 2
           + 2 * (k_rows * 2 * D_FF * 4 + 2 * k_rows * D_FF_PAD * 2 + dn_rows * D_MODEL * 6 + 2 * k_rows * D_MODEL * 6))
    step_rows = lambda r, cols: pl.BlockSpec((r, cols), lambda i, j: (i, 0))
    return pl.pallas_call(
        _in_proj_kernel,
        grid=(n, 1),
        in_specs=[
            step_rows(IN_TM, D_MODEL),
            _resident((1, D_MODEL)),
            _resident((D_MODEL, IN_COLS)),
            pl.BlockSpec((IN_TM, 4 * RET_HEAD_DIM), lambda i, j: (i % tiles_per_seq, 0)),
            step_rows(k_rows, 2 * D_FF),
            step_rows(dn_rows, D_MODEL),
            step_rows(k_rows, D_MODEL),
            step_rows(k_rows, D_MODEL),
        ],
        out_specs=[
            step_rows(IN_TM, IN_COLS),
            step_rows(k_rows, D_FF_PAD),
            step_rows(k_rows, D_FF_PAD),
            step_rows(dn_rows, D_MODEL),
            step_rows(k_rows, D_MODEL),
            step_rows(k_rows, D_MODEL),
        ],
        out_shape=[
            jax.ShapeDtypeStruct((m, IN_COLS), BF16),
            jax.ShapeDtypeStruct((D_MODEL, D_FF_PAD), BF16),
            jax.ShapeDtypeStruct((D_MODEL, D_FF_PAD), BF16),
            jax.ShapeDtypeStruct((D_FF_PAD, D_MODEL), BF16),
            jax.ShapeDtypeStruct((D_MODEL, D_MODEL), BF16),
            jax.ShapeDtypeStruct((D_MODEL, D_MODEL), BF16),
        ],
        scratch_shapes=[pltpu.VMEM((IN_TM, D_MODEL), BF16)],
        compiler_params=_params(blk),
        name="in_proj",
    )(x2, norm1_w, w_in_b, rope, w_up, w_down, w_out, ple_gate_w)


def _mixer_tables(seq):
    c = RET_CHUNK
    pos = np.arange(seq, dtype=np.float64)
    inv_freq = 1.0 / (ROPE_BASE ** (np.arange(0, RET_HEAD_DIM, 2, dtype=np.float64) / RET_HEAD_DIM))
    ang = pos[:, None] * inv_freq[None, :]
    cos, sin = np.cos(ang), np.sin(ang)
    k_scale = RET_HEAD_DIM ** -0.5
    cc = np.concatenate([cos, cos], axis=1)
    ss = np.concatenate([-sin, sin], axis=1)
    rope = np.concatenate([cc, ss, cc * k_scale, ss * k_scale], axis=1)

    gamma = 1.0 - np.exp2(-5.0 - np.arange(RET_HEADS, dtype=np.float64))
    idx = np.arange(c, dtype=np.float64)
    diff = idx[:, None] - idx[None, :]
    decay = np.where(diff[None] >= 0, gamma[:, None, None] ** np.maximum(diff, 0.0)[None], 0.0)
    xi = gamma[None, :] ** (idx[:, None] + 1.0)
    zeta = gamma[None, :] ** (c - 1.0 - idx[:, None])
    xi_t = np.repeat(xi, RET_HEAD_DIM, axis=1)
    zeta_t = np.repeat(zeta, RET_HEAD_DIM, axis=1)
    g_chunk = tuple(float(g) for g in gamma ** c)

    band = np.zeros((N_POOL_GROUPS, c, 2 * c), dtype=np.float32)
    for g, w in enumerate(POOL_WINDOWS):
        for m_ in range(w):
            band[g, np.arange(c), c + np.arange(c) - m_] = 1.0
    t1 = np.concatenate([idx + 1.0, np.full(c, float(max(POOL_WINDOWS)))])
    inv_cnt = np.concatenate([np.repeat(1.0 / np.minimum(t1, float(w))[:, None], POOL_GROUP_DIM, axis=1)
                              for w in POOL_WINDOWS], axis=1)
    return (jnp.asarray(rope, F32), jnp.asarray(decay, F32), jnp.asarray(xi_t, F32),
            jnp.asarray(zeta_t, F32), jnp.asarray(band, BF16), jnp.asarray(inv_cnt, F32), g_chunk)


def _mixer_kernel(g_chunk, n_steps, blocks_per_seq, z_ref, decay_ref, xi_ref, zeta_ref, band_ref, inv_ref,
                  pw_ref, ps_ref, gn_ref, x_ref, wo_ref, n2_ref, h_ref, c_ref, uprev_scr, state_scr, mix_scr):
    c = RET_CHUNK
    hd = RET_HEAD_DIM
    g = pl.program_id(0)
    ci = lax.rem(jnp.minimum(g, n_steps - 1), blocks_per_seq)
    mix_cur = mix_scr.at[0]
    mix_prev = mix_scr.at[1]
    heads = range(RET_HEADS)
    groups = range(N_POOL_GROUPS)
    nt = (((1,), (1,)), ((), ()))
    tn = (((0,), (0,)), ((), ()))

    @pl.when(g == 0)
    def _():
        mix_scr[...] = jnp.zeros_like(mix_scr)

    @pl.when(ci == 0)
    def _():
        uprev_scr[...] = jnp.zeros_like(uprev_scr)
        state_scr[...] = jnp.zeros_like(state_scr)

    n_col = D_MODEL // OUT_TN
    pieces = [(u, n) for u in range(MIX_CHUNKS * c // SUB) for n in range(n_col)]
    assert len(pieces) == 2 * MIX_CHUNKS

    def out_piece(k):
        u, n = pieces[k]
        rows = slice(u * SUB, (u + 1) * SUB)
        cols = slice(n * OUT_TN, (n + 1) * OUT_TN)
        acc = jnp.dot(mix_prev[rows, :], wo_ref[:, cols], preferred_element_type=F32)
        h_ref[rows, cols] = x_ref[rows, cols] + acc
        if n == n_col - 1:
            c_ref[rows, :] = _rms_rows(h_ref[rows, :], n2_ref[...]).astype(BF16)

    uprev_scr[c:, :] = z_ref[0:c, :POOL_WIDTH]
    st = [state_scr[h] for h in heads]

    for t in range(MIX_CHUNKS):
        rows = slice(t * c, (t + 1) * c)

        def zcols(part, h, rows=rows):
            lo = POOL_WIDTH + part * RET_WIDTH + h * hd
            return z_ref[rows, lo:lo + hd]

        q_b = [zcols(0, h) for h in heads]
        k_b = [zcols(1, h) for h in heads]
        q_x = [(q_b[h].astype(F32) * xi_ref[:, h * hd:(h + 1) * hd]).astype(BF16) for h in heads]
        k_z = [(k_b[h].astype(F32) * zeta_ref[:, h * hd:(h + 1) * hd]).astype(BF16) for h in heads]

        def ucat(gp, t=t):
            cols = slice(gp * POOL_GROUP_DIM, (gp + 1) * POOL_GROUP_DIM)
            return uprev_scr[:, cols] if t == 0 else z_ref[(t - 1) * c:(t + 1) * c, cols]

        out_piece(2 * t)
        sc = [lax.dot_general(q_b[h], k_b[h], nt, preferred_element_type=F32) for h in heads]
        kv = [lax.dot_general(k_z[h], zcols(2, h), tn, preferred_element_type=F32) for h in heads]
        wsum = [jnp.dot(band_ref[gp], ucat(gp), preferred_element_type=F32) for gp in groups]

        lhs = [jnp.concatenate([(sc[h] * decay_ref[h]).astype(BF16), q_x[h]], axis=1) for h in heads]
        rhs = [jnp.concatenate([zcols(2, h), st[h].astype(BF16)], axis=0) for h in heads]
        st = [g_chunk[h] * st[h] + kv[h] for h in heads]
        d_b = []
        for gp, w in enumerate(POOL_WINDOWS):
            cols = slice(gp * POOL_GROUP_DIM, (gp + 1) * POOL_GROUP_DIM)
            inv = inv_ref[:, cols] if t == 0 else 1.0 / w
            d_b.append((wsum[gp] * inv - z_ref[rows, cols].astype(F32)).astype(BF16))

        out_piece(2 * t + 1)
        ret = [jnp.dot(lhs[h], rhs[h], preferred_element_type=F32) for h in heads]
        y = [jnp.dot(d_b[gp], pw_ref[gp], preferred_element_type=F32) for gp in groups]

        for gp in groups:
            cols = slice(gp * POOL_GROUP_DIM, (gp + 1) * POOL_GROUP_DIM)
            mix_cur[rows, cols] = (y[gp] * ps_ref[:, cols]).astype(BF16)
        for h in heads:
            hc = slice(h * hd, (h + 1) * hd)
            r = ret[h]
            mu = jnp.mean(r, axis=-1, keepdims=True)
            xc = r - mu
            var = jnp.mean(xc * xc, axis=-1, keepdims=True)
            rn = xc * lax.rsqrt(var + GN_EPS) * gn_ref[:, hc]
            silu = zcols(3, h).astype(F32)
            mix_cur[rows, POOL_WIDTH + h * hd:POOL_WIDTH + (h + 1) * hd] = (silu * rn).astype(BF16)

    for h in heads:
        state_scr[h] = st[h]
    uprev_scr[:c, :] = z_ref[(MIX_CHUNKS - 1) * c:MIX_CHUNKS * c, :POOL_WIDTH]
    mix_prev[...] = mix_cur[...]


def _mixer(z, tables, pool_w_b, pool_scale, ret_gn_w, x2, w_out_b, norm2_w, batch, seq):
    c = RET_CHUNK
    rows = MIX_CHUNKS * c
    blocks_per_seq = seq // rows
    n_steps = batch * blocks_per_seq
    m = batch * seq
    decay, xi_t, zeta_t, band, inv_cnt, g_chunk = tables
    blk = (2 * (rows * IN_COLS * 2 + c * POOL_WIDTH * 4
                + 2 * rows * D_MODEL * 4 + rows * D_MODEL * 2)
           + RET_HEADS * c * c * 4 + 2 * c * RET_WIDTH * 4 + N_POOL_GROUPS * c * 2 * c * 2
           + N_POOL_GROUPS * POOL_GROUP_DIM * POOL_GROUP_DIM * 2 + D_MODEL * D_MODEL * 2
           + 2 * c * POOL_WIDTH * 2 + RET_HEADS * RET_HEAD_DIM * RET_HEAD_DIM * 4 + 2 * rows * D_MODEL * 2)
    cur = lambda g: jnp.minimum(g, n_steps - 1)
    prev_spec = pl.BlockSpec((rows, D_MODEL), lambda g, j: (jnp.maximum(g - 1, 0), 0))
    return pl.pallas_call(
        functools.partial(_mixer_kernel, g_chunk, n_steps, blocks_per_seq),
        grid=(n_steps + 1, 1),
        in_specs=[
            pl.BlockSpec((rows, IN_COLS), lambda g, j: (cur(g), 0)),
            _resident((RET_HEADS, c, c)),
            _resident((c, RET_WIDTH)),
            _resident((c, RET_WIDTH)),
            _resident((N_POOL_GROUPS, c, 2 * c)),
            pl.BlockSpec((c, POOL_WIDTH), lambda g, j: (jnp.minimum(cur(g) % blocks_per_seq, 1), 0)),
            _resident((N_POOL_GROUPS, POOL_GROUP_DIM, POOL_GROUP_DIM)),
            _resident((1, POOL_WIDTH)),
            _resident((1, RET_WIDTH)),
            prev_spec,
            _resident((D_MODEL, D_MODEL)),
            _resident((1, D_MODEL)),
        ],
        out_specs=[prev_spec, prev_spec],
        out_shape=[jax.ShapeDtypeStruct((m, D_MODEL), F32), jax.ShapeDtypeStruct((m, D_MODEL), BF16)],
        scratch_shapes=[
            pltpu.VMEM((2 * c, POOL_WIDTH), BF16),
            pltpu.VMEM((RET_HEADS, RET_HEAD_DIM, RET_HEAD_DIM), F32),
            pltpu.VMEM((2, rows, D_MODEL), BF16),
        ],
        compiler_params=_params(blk),
        name="mixer",
    )(z, decay, xi_t, zeta_t, band, inv_cnt, pool_w_b, pool_scale, ret_gn_w, x2, w_out_b, norm2_w)


FFN_SLABS = FFN_TN // V7X_LANES


def _ffn_kernel(tiles_per_seq, c_ref, wg_ref, wv_ref, cwg_ref, cwv_ref, cbg_ref, cbv_ref, wd_ref,
                o_ref, hg_scr, hv_scr, ug_scr, uv_scr, acc_scr):
    i = pl.program_id(0)
    j = pl.program_id(1)
    hh = V7X_SUBLANES

    @pl.when(jnp.logical_and(i == 0, j == 0))
    def _():
        hg_scr[...] = jnp.zeros_like(hg_scr)
        hv_scr[...] = jnp.zeros_like(hv_scr)
        acc_scr[...] = jnp.zeros_like(acc_scr)

    ug_scr[:, 0:hh, :] = hg_scr[j]
    uv_scr[:, 0:hh, :] = hv_scr[j]

    c_b = c_ref[...]
    ug = jnp.dot(c_b, wg_ref[...], preferred_element_type=F32)
    uv = jnp.dot(c_b, wv_ref[...], preferred_element_type=F32)
    for ct in range(FFN_SLABS):
        lanes = slice(ct * V7X_LANES, (ct + 1) * V7X_LANES)
        ug_scr[ct, hh:hh + FFN_TM, :] = ug[:, lanes]
        uv_scr[ct, hh:hh + FFN_TM, :] = uv[:, lanes]

    def conv(u_scr, cw, cb):
        outs = []
        for ct in range(FFN_SLABS):
            lanes = slice(ct * V7X_LANES, (ct + 1) * V7X_LANES)
            out = cb[:, lanes] + cw[CONV_WIDTH - 1:CONV_WIDTH, lanes] * u_scr[ct, hh:hh + FFN_TM, :]
            for k in range(1, CONV_WIDTH):
                out = out + cw[CONV_WIDTH - 1 - k:CONV_WIDTH - k, lanes] * u_scr[ct, hh - k:hh - k + FFN_TM, :]
            outs.append(out)
        return jnp.concatenate(outs, axis=1)

    gate = conv(ug_scr, cwg_ref[...], cbg_ref[...])
    val = conv(uv_scr, cwv_ref[...], cbv_ref[...])
    act = (gate / (1.0 + jnp.exp(-gate)) * val).astype(BF16)
    total = jnp.where(j == 0, 0.0, acc_scr[...]) + jnp.dot(act, wd_ref[...], preferred_element_type=F32)
    acc_scr[...] = total
    o_ref[...] = total
    keep = (i + 1) % tiles_per_seq != 0
    hg_scr[j] = jnp.where(keep, ug_scr[:, FFN_TM:FFN_TM + hh, :], 0.0)
    hv_scr[j] = jnp.where(keep, uv_scr[:, FFN_TM:FFN_TM + hh, :], 0.0)


def _ffn(c, w_up_gv, conv_w_gv, conv_b_gv, w_down_p, seq):
    m = c.shape[0]
    nj = D_FF_PAD // FFN_TN
    halo = (nj, FFN_SLABS, V7X_SUBLANES, V7X_LANES)
    slab = (FFN_SLABS, V7X_SUBLANES + FFN_TM, V7X_LANES)
    blk = (2 * (FFN_TM * D_MODEL * 2 + FFN_TM * D_MODEL * 4 + 2 * D_MODEL * FFN_TN * 2 + FFN_TN * D_MODEL * 2)
           + 2 * int(np.prod(halo)) * 4 + 2 * int(np.prod(slab)) * 4 + FFN_TM * D_MODEL * 4)
    return pl.pallas_call(
        functools.partial(_ffn_kernel, seq // FFN_TM),
        grid=(m // FFN_TM, nj),
        in_specs=[
            pl.BlockSpec((FFN_TM, D_MODEL), lambda i, j: (i, 0)),
            pl.BlockSpec((D_MODEL, FFN_TN), lambda i, j: (0, j)),
            pl.BlockSpec((D_MODEL, FFN_TN), lambda i, j: (0, j)),
            pl.BlockSpec((CONV_WIDTH, FFN_TN), lambda i, j: (0, j)),
            pl.BlockSpec((CONV_WIDTH, FFN_TN), lambda i, j: (0, j)),
            pl.BlockSpec((1, FFN_TN), lambda i, j: (0, j)),
            pl.BlockSpec((1, FFN_TN), lambda i, j: (0, j)),
            pl.BlockSpec((FFN_TN, D_MODEL), lambda i, j: (j, 0)),
        ],
        out_specs=pl.BlockSpec((FFN_TM, D_MODEL), lambda i, j: (i, 0)),
        out_shape=jax.ShapeDtypeStruct((m, D_MODEL), F32),
        scratch_shapes=[pltpu.VMEM(halo, F32), pltpu.VMEM(halo, F32), pltpu.VMEM(slab, F32), pltpu.VMEM(slab, F32),
                        pltpu.VMEM((FFN_TM, D_MODEL), F32)],
        compiler_params=_params(blk),
        name="ffn",
    )(c, *w_up_gv, *conv_w_gv, *conv_b_gv, w_down_p)


def _ple_kernel(h_ref, f_ref, p_ref, n3_ref, wg_ref, wp_ref, nf_ref, o_ref, e_scr, pb_scr):
    subs = [slice(s * SUB, (s + 1) * SUB) for s in range(PLE_TM // SUB)]
    chunks = [slice(n * PLE_TN, (n + 1) * PLE_TN) for n in range(D_MODEL // PLE_TN)]
    for rows in subs:
        h2 = h_ref[rows, :] + f_ref[rows, :]
        o_ref[rows, :] = h2
        e_scr[rows, :] = _rms_rows(h2, n3_ref[...]).astype(BF16)
        pb_scr[rows, :] = p_ref[rows, :].astype(BF16)
    for rows in subs:
        for cols in chunks:
            gate_logit = jnp.dot(e_scr[rows, :], wg_ref[:, cols], preferred_element_type=F32)
            emb = jnp.dot(pb_scr[rows, :], wp_ref[:, cols], preferred_element_type=F32)
            o_ref[rows, cols] += emb / (1.0 + jnp.exp(-gate_logit))
    for rows in subs:
        o_ref[rows, :] = _rms_rows(o_ref[rows, :], nf_ref[...])


def _ple(h1, f, p2, norm3_w, ple_gate_b, ple_proj_b, final_norm_w):
    m = h1.shape[0]
    ple_dim = p2.shape[1]
    blk = (2 * (3 * PLE_TM * D_MODEL * 4 + PLE_TM * ple_dim * 4) + D_MODEL * D_MODEL * 2 + ple_dim * D_MODEL * 2
           + PLE_TM * D_MODEL * 2 + PLE_TM * ple_dim * 2)
    row_spec = pl.BlockSpec((PLE_TM, D_MODEL), lambda i, j: (i, 0))
    return pl.pallas_call(
        _ple_kernel,
        grid=(m // PLE_TM, 1),
        in_specs=[
            row_spec,
            row_spec,
            pl.BlockSpec((PLE_TM, ple_dim), lambda i, j: (i, 0)),
            _resident((1, D_MODEL)),
            _resident((D_MODEL, D_MODEL)),
            _resident((ple_dim, D_MODEL)),
            _resident((1, D_MODEL)),
        ],
        out_specs=row_spec,
        out_shape=jax.ShapeDtypeStruct((m, D_MODEL), F32),
        scratch_shapes=[pltpu.VMEM((PLE_TM, D_MODEL), BF16), pltpu.VMEM((PLE_TM, ple_dim), BF16)],
        compiler_params=_params(blk),
        name="ple",
    )(h1, f, p2, norm3_w, ple_gate_b, ple_proj_b, final_norm_w)


def _gate_val(a, dtype):
    pad = [(0, 0)] * (a.ndim - 1) + [(0, D_FF_PAD - D_FF)]
    return (jnp.pad(a[..., :D_FF], pad).astype(dtype), jnp.pad(a[..., D_FF:], pad).astype(dtype))


def kernel(x, p, norm1_w, w_in, pool_w, pool_scale, ret_gn_w, w_out, norm2_w, w_up, conv_w, conv_b, w_down,
           norm3_w, ple_gate_w, ple_proj_w, final_norm_w):
    batch, seq, d_model = x.shape
    assert w_in.shape[0] == 1, "one layer per call"
    assert d_model == D_MODEL and seq % (MIX_CHUNKS * RET_CHUNK) == 0 and seq % FFN_TM == 0
    m = batch * seq
    assert m % IN_TM == 0 and seq % IN_TM == 0 and m % PLE_TM == 0
    row = lambda v: v.reshape(1, -1)

    w_in_b = w_in[0].astype(BF16)
    pool_w_b = pool_w[0].astype(BF16)
    conv_w_gv = _gate_val(conv_w[0], F32)
    conv_b_gv = _gate_val(row(conv_b[0]), F32)
    ple_proj_b = ple_proj_w[0].astype(BF16)

    x2 = x.reshape(m, D_MODEL)
    rope, *tables = _mixer_tables(seq)
    z, w_up_g, w_up_v, w_down_p, w_out_b, ple_gate_b = _in_proj(
        x2, row(norm1_w[0]), w_in_b, rope, w_up[0], w_down[0], w_out[0], ple_gate_w[0], seq)
    h1, c = _mixer(z, tables, pool_w_b, row(pool_scale[0]), row(ret_gn_w[0]), x2, w_out_b, row(norm2_w[0]), batch, seq)
    f = _ffn(c, (w_up_g, w_up_v), conv_w_gv, conv_b_gv, w_down_p, seq)
    out = _ple(h1, f, p[0].reshape(m, -1), row(norm3_w[0]), ple_gate_b, ple_proj_b, row(final_norm_w))
    return out.reshape(batch, seq, D_MODEL)
```

```python
import functools

import numpy as np
import jax
import jax.numpy as jnp
from jax import lax
from jax.experimental import pallas as pl
from jax.experimental.pallas import tpu as pltpu

D_MODEL = 2048
POOL_WIDTH = 1024
N_POOL_GROUPS = 4
POOL_GROUP_DIM = 256
POOL_WINDOWS = (2, 4, 8, 16)
RET_WIDTH = 1024
RET_HEADS = 8
RET_HEAD_DIM = 128
ROPE_BASE = 10000.0
IN_COLS = POOL_WIDTH + 4 * RET_WIDTH
Z_Q, Z_K, Z_V, Z_G = (POOL_WIDTH + i * RET_WIDTH for i in range(4))
D_FF = 5504
CONV_WIDTH = 3
NORM_EPS = 1e-6
GN_EPS = 1e-5

V7X_LANES = 128
V7X_SUBLANES = 8
V7X_BF16_SUBLANES = 16
V7X_VMEM_BYTES = 64 * 1024 * 1024
V7X_VMEM_UNSCOPED_BYTES = 6 * 1024 * 1024
TEMPORARIES_BYTES = 16 * 1024 * 1024

RET_CHUNK = 128
MIX_CHUNKS = 4
D_FF_PAD = 5632
FFN_TN = 512
FFN_TM = 1024
SUB = 256
IN_TM = 512
OUT_TN = 512
OUT_TM = 512
PLE_TM = 512
PLE_TN = 512

BF16 = jnp.bfloat16
F32 = jnp.float32


def _vmem_limit(block_bytes):
    return int(min(V7X_VMEM_BYTES - V7X_VMEM_UNSCOPED_BYTES, block_bytes + TEMPORARIES_BYTES))


def _params(block_bytes):
    return pltpu.CompilerParams(
        dimension_semantics=("arbitrary", "arbitrary"),
        vmem_limit_bytes=_vmem_limit(block_bytes),
    )


def _resident(shape):
    return pl.BlockSpec(shape, lambda i, j: (0,) * len(shape), pipeline_mode=pl.Buffered(1))


def _rms_rows(x, w):
    ms = jnp.mean(x * x, axis=-1, keepdims=True)
    return x * lax.rsqrt(ms + NORM_EPS) * w


def _in_proj_kernel(x_ref, nw_ref, w_ref, rope_ref, wup_ref, wdn_ref, wo_ref, pg_ref,
                    z_ref, upg_ref, upv_ref, dn_ref, wob_ref, pgb_ref, a_scr):
    hd = RET_HEAD_DIM
    n_sub = IN_TM // SUB
    for s in range(n_sub):
        rows = slice(s * SUB, (s + 1) * SUB)
        a_scr[rows, :] = _rms_rows(x_ref[rows, :], nw_ref[...]).astype(BF16)

    def proj(rows, lo):
        return jnp.dot(a_scr[rows, :], w_ref[:, lo:lo + RET_WIDTH], preferred_element_type=F32)

    def rotary(t, cc, ss):
        parts = []
        for h in range(RET_HEADS):
            th = t[:, h * hd:(h + 1) * hd]
            parts.append(th * cc + pltpu.roll(th, hd // 2, axis=1) * ss)
        return jnp.concatenate(parts, axis=1)

    for s in range(n_sub):
        rows = slice(s * SUB, (s + 1) * SUB)
        gt = proj(rows, Z_G)
        z_ref[rows, Z_G:Z_G + RET_WIDTH] = (gt / (1.0 + jnp.exp(-gt))).astype(BF16)
        z_ref[rows, Z_Q:Z_Q + RET_WIDTH] = rotary(proj(rows, Z_Q), rope_ref[rows, 0:hd],
                                                  rope_ref[rows, hd:2 * hd]).astype(BF16)
        z_ref[rows, Z_K:Z_K + RET_WIDTH] = rotary(proj(rows, Z_K), rope_ref[rows, 2 * hd:3 * hd],
                                                  rope_ref[rows, 3 * hd:4 * hd]).astype(BF16)
        z_ref[rows, 0:POOL_WIDTH] = proj(rows, 0).astype(BF16)
        z_ref[rows, Z_V:Z_V + RET_WIDTH] = proj(rows, Z_V).astype(BF16)

    zeros = jnp.zeros((upg_ref.shape[0], D_FF_PAD - D_FF), BF16)
    upg_ref[:, :D_FF] = wup_ref[:, :D_FF].astype(BF16)
    upg_ref[:, D_FF:] = zeros
    upv_ref[:, :D_FF] = wup_ref[:, D_FF:].astype(BF16)
    upv_ref[:, D_FF:] = zeros
    dn_rows = dn_ref.shape[0]
    row = pl.program_id(0) * dn_rows + lax.broadcasted_iota(jnp.int32, dn_ref.shape, 0)
    dn_ref[...] = jnp.where(row < D_FF, wdn_ref[...], 0.0).astype(BF16)
    wob_ref[...] = wo_ref[...].astype(BF16)
    pgb_ref[...] = pg_ref[...].astype(BF16)


def _in_proj(x2, norm1_w, w_in_b, rope, w_up, w_down, w_out, ple_gate_w, seq):
    m = x2.shape[0]
    n = m // IN_TM
    tiles_per_seq = seq // IN_TM
    k_rows = D_MODEL // n
    dn_rows = D_FF_PAD // n
    assert D_MODEL % n == 0 and D_FF_PAD % n == 0
    assert k_rows % V7X_BF16_SUBLANES == 0 and dn_rows % V7X_BF16_SUBLANES == 0
    blk = (2 * (IN_TM * D_MODEL * 4 + IN_TM * IN_COLS * 2 + IN_TM * 4 * RET_HEAD_DIM * 4) + D_MODEL * IN_COLS * 2
           + IN_TM * D_MODEL * 2
           + 2 * (k_rows * 2 * D_FF * 4 + 2 * k_rows * D_FF_PAD * 2 + dn_rows * D_MODEL * 6 + 2 * k_rows * D_MODEL * 6))
    step_rows = lambda r, cols: pl.BlockSpec((r, cols), lambda i, j: (i, 0))
    return pl.pallas_call(
        _in_proj_kernel,
        grid=(n, 1),
        in_specs=[
            step_rows(IN_TM, D_MODEL),
            _resident((1, D_MODEL)),
            _resident((D_MODEL, IN_COLS)),
            pl.BlockSpec((IN_TM, 4 * RET_HEAD_DIM), lambda i, j: (i % tiles_per_seq, 0)),
            step_rows(k_rows, 2 * D_FF),
            step_rows(dn_rows, D_MODEL),
            step_rows(k_rows, D_MODEL),
            step_rows(k_rows, D_MODEL),
        ],
        out_specs=[
            step_rows(IN_TM, IN_COLS),
            step_rows(k_rows, D_FF_PAD),
            step_rows(k_rows, D_FF_PAD),
            step_rows(dn_rows, D_MODEL),
            step_rows(k_rows, D_MODEL),
            step_rows(k_rows, D_MODEL),
        ],
        out_shape=[
            jax.ShapeDtypeStruct((m, IN_COLS), BF16),
            jax.ShapeDtypeStruct((D_MODEL, D_FF_PAD), BF16),
            jax.ShapeDtypeStruct((D_MODEL, D_FF_PAD), BF16),
            jax.ShapeDtypeStruct((D_FF_PAD, D_MODEL), BF16),
            jax.ShapeDtypeStruct((D_MODEL, D_MODEL), BF16),
            jax.ShapeDtypeStruct((D_MODEL, D_MODEL), BF16),
        ],
        scratch_shapes=[pltpu.VMEM((IN_TM, D_MODEL), BF16)],
        compiler_params=_params(blk),
        name="in_proj",
    )(x2, norm1_w, w_in_b, rope, w_up, w_down, w_out, ple_gate_w)


def _mixer_tables(seq):
    c = RET_CHUNK
    pos = np.arange(seq, dtype=np.float64)
    inv_freq = 1.0 / (ROPE_BASE ** (np.arange(0, RET_HEAD_DIM, 2, dtype=np.float64) / RET_HEAD_DIM))
    ang = pos[:, None] * inv_freq[None, :]
    cos, sin = np.cos(ang), np.sin(ang)
    k_scale = RET_HEAD_DIM ** -0.5
    cc = np.concatenate([cos, cos], axis=1)
    ss = np.concatenate([-sin, sin], axis=1)
    rope = np.concatenate([cc, ss, cc * k_scale, ss * k_scale], axis=1)

    gamma = 1.0 - np.exp2(-5.0 - np.arange(RET_HEADS, dtype=np.float64))
    idx = np.arange(c, dtype=np.float64)
    diff = idx[:, None] - idx[None, :]
    decay = np.where(diff[None] >= 0, gamma[:, None, None] ** np.maximum(diff, 0.0)[None], 0.0)
    xi = gamma[None, :] ** (idx[:, None] + 1.0)
    zeta = gamma[None, :] ** (c - 1.0 - idx[:, None])
    xi_t = np.repeat(xi, RET_HEAD_DIM, axis=1)
    zeta_t = np.repeat(zeta, RET_HEAD_DIM, axis=1)
    g_chunk = tuple(float(g) for g in gamma ** c)

    band = np.zeros((N_POOL_GROUPS, c, 2 * c), dtype=np.float32)
    for g, w in enumerate(POOL_WINDOWS):
        for m_ in range(w):
            band[g, np.arange(c), c + np.arange(c) - m_] = 1.0
    t1 = np.concatenate([idx + 1.0, np.full(c, float(max(POOL_WINDOWS)))])
    inv_cnt = np.concatenate([np.repeat(1.0 / np.minimum(t1, float(w))[:, None], POOL_GROUP_DIM, axis=1)
                              for w in POOL_WINDOWS], axis=1)
    return (jnp.asarray(rope, F32), jnp.asarray(decay, F32), jnp.asarray(xi_t, F32),
            jnp.asarray(zeta_t, F32), jnp.asarray(band, BF16), jnp.asarray(inv_cnt, F32), g_chunk)


def _mixer_kernel(g_chunk, z_ref, decay_ref, xi_ref, zeta_ref, band_ref, inv_ref,
                  pw_ref, ps_ref, gn_ref, mix_cur, uprev_scr, state_scr):
    c = RET_CHUNK
    hd = RET_HEAD_DIM
    ci = pl.program_id(1)
    heads = range(RET_HEADS)
    groups = range(N_POOL_GROUPS)
    nt = (((1,), (1,)), ((), ()))
    tn = (((0,), (0,)), ((), ()))

    @pl.when(ci == 0)
    def _():
        uprev_scr[...] = jnp.zeros_like(uprev_scr)
        state_scr[...] = jnp.zeros_like(state_scr)

    uprev_scr[c:, :] = z_ref[0:c, :POOL_WIDTH]
    st = [state_scr[h] for h in heads]

    for t in range(MIX_CHUNKS):
        rows = slice(t * c, (t + 1) * c)

        def zcols(part, h, rows=rows):
            lo = POOL_WIDTH + part * RET_WIDTH + h * hd
            return z_ref[rows, lo:lo + hd]

        q_b = [zcols(0, h) for h in heads]
        k_b = [zcols(1, h) for h in heads]
        q_x = [(q_b[h].astype(F32) * xi_ref[:, h * hd:(h + 1) * hd]).astype(BF16) for h in heads]
        k_z = [(k_b[h].astype(F32) * zeta_ref[:, h * hd:(h + 1) * hd]).astype(BF16) for h in heads]

        def ucat(gp, t=t):
            cols = slice(gp * POOL_GROUP_DIM, (gp + 1) * POOL_GROUP_DIM)
            return uprev_scr[:, cols] if t == 0 else z_ref[(t - 1) * c:(t + 1) * c, cols]

        sc = [lax.dot_general(q_b[h], k_b[h], nt, preferred_element_type=F32) for h in heads]
        kv = [lax.dot_general(k_z[h], zcols(2, h), tn, preferred_element_type=F32) for h in heads]
        wsum = [jnp.dot(band_ref[gp], ucat(gp), preferred_element_type=F32) for gp in groups]

        lhs = [jnp.concatenate([(sc[h] * decay_ref[h]).astype(BF16), q_x[h]], axis=1) for h in heads]
        rhs = [jnp.concatenate([zcols(2, h), st[h].astype(BF16)], axis=0) for h in heads]
        st = [g_chunk[h] * st[h] + kv[h] for h in heads]
        d_b = []
        for gp, w in enumerate(POOL_WINDOWS):
            cols = slice(gp * POOL_GROUP_DIM, (gp + 1) * POOL_GROUP_DIM)
            inv = inv_ref[:, cols] if t == 0 else 1.0 / w
            d_b.append((wsum[gp] * inv - z_ref[rows, cols].astype(F32)).astype(BF16))

        ret = [jnp.dot(lhs[h], rhs[h], preferred_element_type=F32) for h in heads]
        y = [jnp.dot(d_b[gp], pw_ref[gp], preferred_element_type=F32) for gp in groups]

        for gp in groups:
            cols = slice(gp * POOL_GROUP_DIM, (gp + 1) * POOL_GROUP_DIM)
            mix_cur[rows, cols] = (y[gp] * ps_ref[:, cols]).astype(BF16)
        for h in heads:
            hc = slice(h * hd, (h + 1) * hd)
            r = ret[h]
            mu = jnp.mean(r, axis=-1, keepdims=True)
            xc = r - mu
            var = jnp.mean(xc * xc, axis=-1, keepdims=True)
            rn = xc * lax.rsqrt(var + GN_EPS) * gn_ref[:, hc]
            silu = zcols(3, h).astype(F32)
            mix_cur[rows, POOL_WIDTH + h * hd:POOL_WIDTH + (h + 1) * hd] = (silu * rn).astype(BF16)

    for h in heads:
        state_scr[h] = st[h]
    uprev_scr[:c, :] = z_ref[(MIX_CHUNKS - 1) * c:MIX_CHUNKS * c, :POOL_WIDTH]


def _mixer(z, tables, pool_w_b, pool_scale, ret_gn_w, batch, seq):
    c = RET_CHUNK
    rows = MIX_CHUNKS * c
    blocks_per_seq = seq // rows
    decay, xi_t, zeta_t, band, inv_cnt, g_chunk = tables
    blk = (2 * (rows * IN_COLS * 2 + c * POOL_WIDTH * 4 + rows * D_MODEL * 2)
           + RET_HEADS * c * c * 4 + 2 * c * RET_WIDTH * 4 + N_POOL_GROUPS * c * 2 * c * 2
           + N_POOL_GROUPS * POOL_GROUP_DIM * POOL_GROUP_DIM * 2
           + 2 * c * POOL_WIDTH * 2 + RET_HEADS * RET_HEAD_DIM * RET_HEAD_DIM * 4)
    return pl.pallas_call(
        functools.partial(_mixer_kernel, g_chunk),
        grid=(batch, blocks_per_seq),
        in_specs=[
            pl.BlockSpec((rows, IN_COLS), lambda b, ci: (b * blocks_per_seq + ci, 0)),
            _resident((RET_HEADS, c, c)),
            _resident((c, RET_WIDTH)),
            _resident((c, RET_WIDTH)),
            _resident((N_POOL_GROUPS, c, 2 * c)),
            pl.BlockSpec((c, POOL_WIDTH), lambda b, ci: (jnp.minimum(ci, 1), 0)),
            _resident((N_POOL_GROUPS, POOL_GROUP_DIM, POOL_GROUP_DIM)),
            _resident((1, POOL_WIDTH)),
            _resident((1, RET_WIDTH)),
        ],
        out_specs=pl.BlockSpec((rows, D_MODEL), lambda b, ci: (b * blocks_per_seq + ci, 0)),
        out_shape=jax.ShapeDtypeStruct((batch * seq, D_MODEL), BF16),
        scratch_shapes=[
            pltpu.VMEM((2 * c, POOL_WIDTH), BF16),
            pltpu.VMEM((RET_HEADS, RET_HEAD_DIM, RET_HEAD_DIM), F32),
        ],
        compiler_params=_params(blk),
        name="mixer",
    )(z, decay, xi_t, zeta_t, band, inv_cnt, pool_w_b, pool_scale, ret_gn_w)


def _out_proj_kernel(mix_ref, w_ref, x_ref, nw_ref, h_ref, c_ref):
    for rows in [slice(s * SUB, (s + 1) * SUB) for s in range(OUT_TM // SUB)]:
        for n in range(D_MODEL // OUT_TN):
            cols = slice(n * OUT_TN, (n + 1) * OUT_TN)
            h_ref[rows, cols] = x_ref[rows, cols] + jnp.dot(mix_ref[rows, :], w_ref[:, cols],
                                                            preferred_element_type=F32)
        c_ref[rows, :] = _rms_rows(h_ref[rows, :], nw_ref[...]).astype(BF16)


def _out_proj(mix, w_out_b, x2, norm2_w):
    m = x2.shape[0]
    tm = OUT_TM
    blk = 2 * (tm * D_MODEL * 2 + 2 * tm * D_MODEL * 4 + tm * D_MODEL * 2) + D_MODEL * D_MODEL * 2
    row_spec = pl.BlockSpec((tm, D_MODEL), lambda i, j: (i, 0))
    return pl.pallas_call(
        _out_proj_kernel,
        grid=(m // tm, 1),
        in_specs=[row_spec, _resident((D_MODEL, D_MODEL)), row_spec, _resident((1, D_MODEL))],
        out_specs=[row_spec, row_spec],
        out_shape=[jax.ShapeDtypeStruct((m, D_MODEL), F32), jax.ShapeDtypeStruct((m, D_MODEL), BF16)],
        compiler_params=_params(blk),
        name="out_proj",
    )(mix, w_out_b, x2, norm2_w)


FFN_SLABS = FFN_TN // V7X_LANES


def _ffn_kernel(tiles_per_seq, c_ref, wg_ref, wv_ref, cwg_ref, cwv_ref, cbg_ref, cbv_ref, wd_ref,
                o_ref, hg_scr, hv_scr, ug_scr, uv_scr, acc_scr):
    i = pl.program_id(0)
    j = pl.program_id(1)
    hh = V7X_SUBLANES

    @pl.when(jnp.logical_and(i == 0, j == 0))
    def _():
        hg_scr[...] = jnp.zeros_like(hg_scr)
        hv_scr[...] = jnp.zeros_like(hv_scr)
        acc_scr[...] = jnp.zeros_like(acc_scr)

    ug_scr[:, 0:hh, :] = hg_scr[j]
    uv_scr[:, 0:hh, :] = hv_scr[j]

    c_b = c_ref[...]
    ug = jnp.dot(c_b, wg_ref[...], preferred_element_type=F32)
    uv = jnp.dot(c_b, wv_ref[...], preferred_element_type=F32)
    for ct in range(FFN_SLABS):
        lanes = slice(ct * V7X_LANES, (ct + 1) * V7X_LANES)
        ug_scr[ct, hh:hh + FFN_TM, :] = ug[:, lanes]
        uv_scr[ct, hh:hh + FFN_TM, :] = uv[:, lanes]

    def conv(u_scr, cw, cb):
        outs = []
        for ct in range(FFN_SLABS):
            lanes = slice(ct * V7X_LANES, (ct + 1) * V7X_LANES)
            out = cb[:, lanes] + cw[CONV_WIDTH - 1:CONV_WIDTH, lanes] * u_scr[ct, hh:hh + FFN_TM, :]
            for k in range(1, CONV_WIDTH):
                out = out + cw[CONV_WIDTH - 1 - k:CONV_WIDTH - k, lanes] * u_scr[ct, hh - k:hh - k + FFN_TM, :]
            outs.append(out)
        return jnp.concatenate(outs, axis=1)

    gate = conv(ug_scr, cwg_ref[...], cbg_ref[...])
    val = conv(uv_scr, cwv_ref[...], cbv_ref[...])
    act = (gate / (1.0 + jnp.exp(-gate)) * val).astype(BF16)
    total = jnp.where(j == 0, 0.0, acc_scr[...]) + jnp.dot(act, wd_ref[...], preferred_element_type=F32)
    acc_scr[...] = total
    o_ref[...] = total
    keep = (i + 1) % tiles_per_seq != 0
    hg_scr[j] = jnp.where(keep, ug_scr[:, FFN_TM:FFN_TM + hh, :], 0.0)
    hv_scr[j] = jnp.where(keep, uv_scr[:, FFN_TM:FFN_TM + hh, :], 0.0)


def _ffn(c, w_up_gv, conv_w_gv, conv_b_gv, w_down_p, seq):
    m = c.shape[0]
    nj = D_FF_PAD // FFN_TN
    halo = (nj, FFN_SLABS, V7X_SUBLANES, V7X_LANES)
    slab = (FFN_SLABS, V7X_SUBLANES + FFN_TM, V7X_LANES)
    blk = (2 * (FFN_TM * D_MODEL * 2 + FFN_TM * D_MODEL * 4 + 2 * D_MODEL * FFN_TN * 2 + FFN_TN * D_MODEL * 2)
           + 2 * int(np.prod(halo)) * 4 + 2 * int(np.prod(slab)) * 4 + FFN_TM * D_MODEL * 4)
    return pl.pallas_call(
        functools.partial(_ffn_kernel, seq // FFN_TM),
        grid=(m // FFN_TM, nj),
        in_specs=[
            pl.BlockSpec((FFN_TM, D_MODEL), lambda i, j: (i, 0)),
            pl.BlockSpec((D_MODEL, FFN_TN), lambda i, j: (0, j)),
            pl.BlockSpec((D_MODEL, FFN_TN), lambda i, j: (0, j)),
            pl.BlockSpec((CONV_WIDTH, FFN_TN), lambda i, j: (0, j)),
            pl.BlockSpec((CONV_WIDTH, FFN_TN), lambda i, j: (0, j)),
            pl.BlockSpec((1, FFN_TN), lambda i, j: (0, j)),
            pl.BlockSpec((1, FFN_TN), lambda i, j: (0, j)),
            pl.BlockSpec((FFN_TN, D_MODEL), lambda i, j: (j, 0)),
        ],
        out_specs=pl.BlockSpec((FFN_TM, D_MODEL), lambda i, j: (i, 0)),
        out_shape=jax.ShapeDtypeStruct((m, D_MODEL), F32),
        scratch_shapes=[pltpu.VMEM(halo, F32), pltpu.VMEM(halo, F32), pltpu.VMEM(slab, F32), pltpu.VMEM(slab, F32),
                        pltpu.VMEM((FFN_TM, D_MODEL), F32)],
        compiler_params=_params(blk),
        name="ffn",
    )(c, *w_up_gv, *conv_w_gv, *conv_b_gv, w_down_p)


def _ple_kernel(h_ref, f_ref, p_ref, n3_ref, wg_ref, wp_ref, nf_ref, o_ref, e_scr, pb_scr):
    subs = [slice(s * SUB, (s + 1) * SUB) for s in range(PLE_TM // SUB)]
    chunks = [slice(n * PLE_TN, (n + 1) * PLE_TN) for n in range(D_MODEL // PLE_TN)]
    for rows in subs:
        h2 = h_ref[rows, :] + f_ref[rows, :]
        o_ref[rows, :] = h2
        e_scr[rows, :] = _rms_rows(h2, n3_ref[...]).astype(BF16)
        pb_scr[rows, :] = p_ref[rows, :].astype(BF16)
    for rows in subs:
        for cols in chunks:
            gate_logit = jnp.dot(e_scr[rows, :], wg_ref[:, cols], preferred_element_type=F32)
            emb = jnp.dot(pb_scr[rows, :], wp_ref[:, cols], preferred_element_type=F32)
            o_ref[rows, cols] += emb / (1.0 + jnp.exp(-gate_logit))
    for rows in subs:
        o_ref[rows, :] = _rms_rows(o_ref[rows, :], nf_ref[...])


def _ple(h1, f, p2, norm3_w, ple_gate_b, ple_proj_b, final_norm_w):
    m = h1.shape[0]
    ple_dim = p2.shape[1]
    blk = (2 * (3 * PLE_TM * D_MODEL * 4 + PLE_TM * ple_dim * 4) + D_MODEL * D_MODEL * 2 + ple_dim * D_MODEL * 2
           + PLE_TM * D_MODEL * 2 + PLE_TM * ple_dim * 2)
    row_spec = pl.BlockSpec((PLE_TM, D_MODEL), lambda i, j: (i, 0))
    return pl.pallas_call(
        _ple_kernel,
        grid=(m // PLE_TM, 1),
        in_specs=[
            row_spec,
            row_spec,
            pl.BlockSpec((PLE_TM, ple_dim), lambda i, j: (i, 0)),
            _resident((1, D_MODEL)),
            _resident((D_MODEL, D_MODEL)),
            _resident((ple_dim, D_MODEL)),
            _resident((1, D_MODEL)),
        ],
        out_specs=row_spec,
        out_shape=jax.ShapeDtypeStruct((m, D_MODEL), F32),
        scratch_shapes=[pltpu.VMEM((PLE_TM, D_MODEL), BF16), pltpu.VMEM((PLE_TM, ple_dim), BF16)],
        compiler_params=_params(blk),
        name="ple",
    )(h1, f, p2, norm3_w, ple_gate_b, ple_proj_b, final_norm_w)


def _gate_val(a, dtype):
    pad = [(0, 0)] * (a.ndim - 1) + [(0, D_FF_PAD - D_FF)]
    return (jnp.pad(a[..., :D_FF], pad).astype(dtype), jnp.pad(a[..., D_FF:], pad).astype(dtype))


def kernel(x, p, norm1_w, w_in, pool_w, pool_scale, ret_gn_w, w_out, norm2_w, w_up, conv_w, conv_b, w_down,
           norm3_w, ple_gate_w, ple_proj_w, final_norm_w):
    batch, seq, d_model = x.shape
    assert w_in.shape[0] == 1, "one layer per call"
    assert d_model == D_MODEL and seq % (MIX_CHUNKS * RET_CHUNK) == 0 and seq % FFN_TM == 0
    m = batch * seq
    assert m % IN_TM == 0 and seq % IN_TM == 0 and m % OUT_TM == 0 and m % PLE_TM == 0
    row = lambda v: v.reshape(1, -1)

    w_in_b = w_in[0].astype(BF16)
    pool_w_b = pool_w[0].astype(BF16)
    conv_w_gv = _gate_val(conv_w[0], F32)
    conv_b_gv = _gate_val(row(conv_b[0]), F32)
    ple_proj_b = ple_proj_w[0].astype(BF16)

    x2 = x.reshape(m, D_MODEL)
    rope, *tables = _mixer_tables(seq)
    z, w_up_g, w_up_v, w_down_p, w_out_b, ple_gate_b = _in_proj(
        x2, row(norm1_w[0]), w_in_b, rope, w_up[0], w_down[0], w_out[0], ple_gate_w[0], seq)
    mix = _mixer(z, tables, pool_w_b, row(pool_scale[0]), row(ret_gn_w[0]), batch, seq)
    h1, c = _out_proj(mix, w_out_b, x2, row(norm2_w[0]))
    f = _ffn(c, (w_up_g, w_up_v), conv_w_gv, conv_b_gv, w_down_p, seq)
    out = _ple(h1, f, p[0].reshape(m, -1), row(norm3_w[0]), ple_gate_b, ple_proj_b, row(final_norm_w))
    return out.reshape(batch, seq, D_MODEL)
```

```python
import functools

import numpy as np
import jax
import jax.numpy as jnp
from jax import lax
from jax.experimental import pallas as pl
from jax.experimental.pallas import tpu as pltpu

D_MODEL = 2048
POOL_WIDTH = 1024
N_POOL_GROUPS = 4
POOL_GROUP_DIM = 256
POOL_WINDOWS = (2, 4, 8, 16)
RET_WIDTH = 1024
RET_HEADS = 8
RET_HEAD_DIM = 128
ROPE_BASE = 10000.0
IN_COLS = POOL_WIDTH + 4 * RET_WIDTH
Z_Q, Z_K, Z_V, Z_G = (POOL_WIDTH + i * RET_WIDTH for i in range(4))
D_FF = 5504
CONV_WIDTH = 3
NORM_EPS = 1e-6
GN_EPS = 1e-5

V7X_LANES = 128
V7X_SUBLANES = 8
V7X_BF16_SUBLANES = 16
V7X_VMEM_BYTES = 64 * 1024 * 1024
V7X_VMEM_UNSCOPED_BYTES = 6 * 1024 * 1024
TEMPORARIES_BYTES = 16 * 1024 * 1024

RET_CHUNK = 128
MIX_CHUNKS = 4
D_FF_PAD = 5632
FFN_TN = 512
FFN_TM = 1024
SUB = 256
IN_TM = 512
OUT_TN = 512
PLE_TM = 512
PLE_TN = 512

BF16 = jnp.bfloat16
F32 = jnp.float32


def _vmem_limit(block_bytes):
    return int(min(V7X_VMEM_BYTES - V7X_VMEM_UNSCOPED_BYTES, block_bytes + TEMPORARIES_BYTES))


def _params(block_bytes):
    return pltpu.CompilerParams(
        dimension_semantics=("arbitrary", "arbitrary"),
        vmem_limit_bytes=_vmem_limit(block_bytes),
    )


def _resident(shape):
    return pl.BlockSpec(shape, lambda i, j: (0,) * len(shape), pipeline_mode=pl.Buffered(1))


def _rms_rows(x, w):
    ms = jnp.mean(x * x, axis=-1, keepdims=True)
    return x * lax.rsqrt(ms + NORM_EPS) * w


def _in_proj_kernel(x_ref, nw_ref, w_ref, rope_ref, wup_ref, wdn_ref, wo_ref, pg_ref,
                    z_ref, upg_ref, upv_ref, dn_ref, wob_ref, pgb_ref, a_scr):
    hd = RET_HEAD_DIM
    n_sub = IN_TM // SUB
    for s in range(n_sub):
        rows = slice(s * SUB, (s + 1) * SUB)
        a_scr[rows, :] = _rms_rows(x_ref[rows, :], nw_ref[...]).astype(BF16)

    def proj(rows, lo):
        return jnp.dot(a_scr[rows, :], w_ref[:, lo:lo + RET_WIDTH], preferred_element_type=F32)

    def rotary(t, cc, ss):
        parts = []
        for h in range(RET_HEADS):
            th = t[:, h * hd:(h + 1) * hd]
            parts.append(th * cc + pltpu.roll(th, hd // 2, axis=1) * ss)
        return jnp.concatenate(parts, axis=1)

    for s in range(n_sub):
        rows = slice(s * SUB, (s + 1) * SUB)
        gt = proj(rows, Z_G)
        z_ref[rows, Z_G:Z_G + RET_WIDTH] = (gt / (1.0 + jnp.exp(-gt))).astype(BF16)
        z_ref[rows, Z_Q:Z_Q + RET_WIDTH] = rotary(proj(rows, Z_Q), rope_ref[rows, 0:hd],
                                                  rope_ref[rows, hd:2 * hd]).astype(BF16)
        z_ref[rows, Z_K:Z_K + RET_WIDTH] = rotary(proj(rows, Z_K), rope_ref[rows, 2 * hd:3 * hd],
                                                  rope_ref[rows, 3 * hd:4 * hd]).astype(BF16)
        z_ref[rows, 0:POOL_WIDTH] = proj(rows, 0).astype(BF16)
        z_ref[rows, Z_V:Z_V + RET_WIDTH] = proj(rows, Z_V).astype(BF16)

    zeros = jnp.zeros((upg_ref.shape[0], D_FF_PAD - D_FF), BF16)
    upg_ref[:, :D_FF] = wup_ref[:, :D_FF].astype(BF16)
    upg_ref[:, D_FF:] = zeros
    upv_ref[:, :D_FF] = wup_ref[:, D_FF:].astype(BF16)
    upv_ref[:, D_FF:] = zeros
    dn_rows = dn_ref.shape[0]
    row = pl.program_id(0) * dn_rows + lax.broadcasted_iota(jnp.int32, dn_ref.shape, 0)
    dn_ref[...] = jnp.where(row < D_FF, wdn_ref[...], 0.0).astype(BF16)
    wob_ref[...] = wo_ref[...].astype(BF16)
    pgb_ref[...] = pg_ref[...].astype(BF16)


def _in_proj(x2, norm1_w, w_in_b, rope, w_up, w_down, w_out, ple_gate_w, seq):
    m = x2.shape[0]
    n = m // IN_TM
    tiles_per_seq = seq // IN_TM
    k_rows = D_MODEL // n
    dn_rows = D_FF_PAD // n
    assert D_MODEL % n == 0 and D_FF_PAD % n == 0
    assert k_rows % V7X_BF16_SUBLANES == 0 and dn_rows % V7X_BF16_SUBLANES == 0
    blk = (2 * (IN_TM * D_MODEL * 4 + IN_TM * IN_COLS * 2 + IN_TM * 4 * RET_HEAD_DIM * 4) + D_MODEL * IN_COLS * 2
           + IN_TM * D_MODEL * 2
           + 2 * (k_rows * 2 * D_FF * 4 + 2 * k_rows * D_FF_PAD * 2 + dn_rows * D_MODEL * 6 + 2 * k_rows * D_MODEL * 6))
    step_rows = lambda r, cols: pl.BlockSpec((r, cols), lambda i, j: (i, 0))
    return pl.pallas_call(
        _in_proj_kernel,
        grid=(n, 1),
        in_specs=[
            step_rows(IN_TM, D_MODEL),
            _resident((1, D_MODEL)),
            _resident((D_MODEL, IN_COLS)),
            pl.BlockSpec((IN_TM, 4 * RET_HEAD_DIM), lambda i, j: (i % tiles_per_seq, 0)),
            step_rows(k_rows, 2 * D_FF),
            step_rows(dn_rows, D_MODEL),
            step_rows(k_rows, D_MODEL),
            step_rows(k_rows, D_MODEL),
        ],
        out_specs=[
            step_rows(IN_TM, IN_COLS),
            step_rows(k_rows, D_FF_PAD),
            step_rows(k_rows, D_FF_PAD),
            step_rows(dn_rows, D_MODEL),
            step_rows(k_rows, D_MODEL),
            step_rows(k_rows, D_MODEL),
        ],
        out_shape=[
            jax.ShapeDtypeStruct((m, IN_COLS), BF16),
            jax.ShapeDtypeStruct((D_MODEL, D_FF_PAD), BF16),
            jax.ShapeDtypeStruct((D_MODEL, D_FF_PAD), BF16),
            jax.ShapeDtypeStruct((D_FF_PAD, D_MODEL), BF16),
            jax.ShapeDtypeStruct((D_MODEL, D_MODEL), BF16),
            jax.ShapeDtypeStruct((D_MODEL, D_MODEL), BF16),
        ],
        scratch_shapes=[pltpu.VMEM((IN_TM, D_MODEL), BF16)],
        compiler_params=_params(blk),
        name="in_proj",
    )(x2, norm1_w, w_in_b, rope, w_up, w_down, w_out, ple_gate_w)


def _mixer_tables(seq):
    c = RET_CHUNK
    pos = np.arange(seq, dtype=np.float64)
    inv_freq = 1.0 / (ROPE_BASE ** (np.arange(0, RET_HEAD_DIM, 2, dtype=np.float64) / RET_HEAD_DIM))
    ang = pos[:, None] * inv_freq[None, :]
    cos, sin = np.cos(ang), np.sin(ang)
    k_scale = RET_HEAD_DIM ** -0.5
    cc = np.concatenate([cos, cos], axis=1)
    ss = np.concatenate([-sin, sin], axis=1)
    rope = np.concatenate([cc, ss, cc * k_scale, ss * k_scale], axis=1)

    gamma = 1.0 - np.exp2(-5.0 - np.arange(RET_HEADS, dtype=np.float64))
    idx = np.arange(c, dtype=np.float64)
    diff = idx[:, None] - idx[None, :]
    decay = np.where(diff[None] >= 0, gamma[:, None, None] ** np.maximum(diff, 0.0)[None], 0.0)
    xi = gamma[None, :] ** (idx[:, None] + 1.0)
    zeta = gamma[None, :] ** (c - 1.0 - idx[:, None])
    xi_t = np.repeat(xi, RET_HEAD_DIM, axis=1)
    zeta_t = np.repeat(zeta, RET_HEAD_DIM, axis=1)
    g_chunk = tuple(float(g) for g in gamma ** c)

    band = np.zeros((N_POOL_GROUPS, c, 2 * c), dtype=np.float32)
    for g, w in enumerate(POOL_WINDOWS):
        for m_ in range(w):
            band[g, np.arange(c), c + np.arange(c) - m_] = 1.0
    t1 = np.concatenate([idx + 1.0, np.full(c, float(max(POOL_WINDOWS)))])
    inv_cnt = np.concatenate([np.repeat(1.0 / np.minimum(t1, float(w))[:, None], POOL_GROUP_DIM, axis=1)
                              for w in POOL_WINDOWS], axis=1)
    return (jnp.asarray(rope, F32), jnp.asarray(decay, F32), jnp.asarray(xi_t, F32),
            jnp.asarray(zeta_t, F32), jnp.asarray(band, BF16), jnp.asarray(inv_cnt, F32), g_chunk)


def _mixer_kernel(g_chunk, n_steps, blocks_per_seq, z_ref, decay_ref, xi_ref, zeta_ref, band_ref, inv_ref,
                  pw_ref, ps_ref, gn_ref, x_ref, wo_ref, n2_ref, h_ref, c_ref, uprev_scr, state_scr, mix_scr):
    c = RET_CHUNK
    hd = RET_HEAD_DIM
    g = pl.program_id(0)
    ci = lax.rem(jnp.minimum(g, n_steps - 1), blocks_per_seq)
    mix_cur = mix_scr.at[0]
    mix_prev = mix_scr.at[1]
    heads = range(RET_HEADS)
    groups = range(N_POOL_GROUPS)
    nt = (((1,), (1,)), ((), ()))
    tn = (((0,), (0,)), ((), ()))

    @pl.when(g == 0)
    def _():
        mix_scr[...] = jnp.zeros_like(mix_scr)
        uprev_scr[...] = jnp.zeros_like(uprev_scr)
        state_scr[...] = jnp.zeros_like(state_scr)

    fresh = ci == 0
    uprev_scr[:c, :] = jnp.where(fresh, jnp.zeros((c, POOL_WIDTH), BF16), uprev_scr[:c, :])

    n_col = D_MODEL // OUT_TN
    pieces = [(u, n) for u in range(MIX_CHUNKS * c // SUB) for n in range(n_col)]
    assert len(pieces) == 2 * MIX_CHUNKS

    def out_piece(k):
        u, n = pieces[k]
        rows = slice(u * SUB, (u + 1) * SUB)
        cols = slice(n * OUT_TN, (n + 1) * OUT_TN)
        acc = jnp.dot(mix_prev[rows, :], wo_ref[:, cols], preferred_element_type=F32)
        h_ref[rows, cols] = x_ref[rows, cols] + acc
        if n == n_col - 1:
            c_ref[rows, :] = _rms_rows(h_ref[rows, :], n2_ref[...]).astype(BF16)

    uprev_scr[c:, :] = z_ref[0:c, :POOL_WIDTH]
    st = [jnp.where(fresh, 0.0, state_scr[h]) for h in heads]

    for t in range(MIX_CHUNKS):
        rows = slice(t * c, (t + 1) * c)

        def zcols(part, h, rows=rows):
            lo = POOL_WIDTH + part * RET_WIDTH + h * hd
            return z_ref[rows, lo:lo + hd]

        q_b = [zcols(0, h) for h in heads]
        k_b = [zcols(1, h) for h in heads]
        q_x = [(q_b[h].astype(F32) * xi_ref[:, h * hd:(h + 1) * hd]).astype(BF16) for h in heads]
        k_z = [(k_b[h].astype(F32) * zeta_ref[:, h * hd:(h + 1) * hd]).astype(BF16) for h in heads]

        def ucat(gp, t=t):
            cols = slice(gp * POOL_GROUP_DIM, (gp + 1) * POOL_GROUP_DIM)
            return uprev_scr[:, cols] if t == 0 else z_ref[(t - 1) * c:(t + 1) * c, cols]

        out_piece(2 * t)
        sc = [lax.dot_general(q_b[h], k_b[h], nt, preferred_element_type=F32) for h in heads]
        kv = [lax.dot_general(k_z[h], zcols(2, h), tn, preferred_element_type=F32) for h in heads]
        wsum = [jnp.dot(band_ref[gp], ucat(gp), preferred_element_type=F32) for gp in groups]

        lhs = [jnp.concatenate([(sc[h] * decay_ref[h]).astype(BF16), q_x[h]], axis=1) for h in heads]
        rhs = [jnp.concatenate([zcols(2, h), st[h].astype(BF16)], axis=0) for h in heads]
        st = [g_chunk[h] * st[h] + kv[h] for h in heads]
        d_b = []
        for gp, w in enumerate(POOL_WINDOWS):
            cols = slice(gp * POOL_GROUP_DIM, (gp + 1) * POOL_GROUP_DIM)
            inv = inv_ref[:, cols] if t == 0 else 1.0 / w
            d_b.append((wsum[gp] * inv - z_ref[rows, cols].astype(F32)).astype(BF16))

        out_piece(2 * t + 1)
        ret = [jnp.dot(lhs[h], rhs[h], preferred_element_type=F32) for h in heads]
        y = [jnp.dot(d_b[gp], pw_ref[gp], preferred_element_type=F32) for gp in groups]

        for gp in groups:
            cols = slice(gp * POOL_GROUP_DIM, (gp + 1) * POOL_GROUP_DIM)
            mix_cur[rows, cols] = (y[gp] * ps_ref[:, cols]).astype(BF16)
        for h in heads:
            hc = slice(h * hd, (h + 1) * hd)
            r = ret[h]
            mu = jnp.mean(r, axis=-1, keepdims=True)
            xc = r - mu
            var = jnp.mean(xc * xc, axis=-1, keepdims=True)
            rn = xc * lax.rsqrt(var + GN_EPS) * gn_ref[:, hc]
            silu = zcols(3, h).astype(F32)
            mix_cur[rows, POOL_WIDTH + h * hd:POOL_WIDTH + (h + 1) * hd] = (silu * rn).astype(BF16)

    for h in heads:
        state_scr[h] = st[h]
    uprev_scr[:c, :] = z_ref[(MIX_CHUNKS - 1) * c:MIX_CHUNKS * c, :POOL_WIDTH]
    mix_prev[...] = mix_cur[...]


def _mixer(z, tables, pool_w_b, pool_scale, ret_gn_w, x2, w_out_b, norm2_w, batch, seq):
    c = RET_CHUNK
    rows = MIX_CHUNKS * c
    blocks_per_seq = seq // rows
    n_steps = batch * blocks_per_seq
    m = batch * seq
    decay, xi_t, zeta_t, band, inv_cnt, g_chunk = tables
    blk = (2 * (rows * IN_COLS * 2 + c * POOL_WIDTH * 4
                + 2 * rows * D_MODEL * 4 + rows * D_MODEL * 2)
           + RET_HEADS * c * c * 4 + 2 * c * RET_WIDTH * 4 + N_POOL_GROUPS * c * 2 * c * 2
           + N_POOL_GROUPS * POOL_GROUP_DIM * POOL_GROUP_DIM * 2 + D_MODEL * D_MODEL * 2
           + 2 * c * POOL_WIDTH * 2 + RET_HEADS * RET_HEAD_DIM * RET_HEAD_DIM * 4 + 2 * rows * D_MODEL * 2)
    cur = lambda g: jnp.minimum(g, n_steps - 1)
    prev_spec = pl.BlockSpec((rows, D_MODEL), lambda g, j: (jnp.maximum(g - 1, 0), 0))
    return pl.pallas_call(
        functools.partial(_mixer_kernel, g_chunk, n_steps, blocks_per_seq),
        grid=(n_steps + 1, 1),
        in_specs=[
            pl.BlockSpec((rows, IN_COLS), lambda g, j: (cur(g), 0)),
            _resident((RET_HEADS, c, c)),
            _resident((c, RET_WIDTH)),
            _resident((c, RET_WIDTH)),
            _resident((N_POOL_GROUPS, c, 2 * c)),
            pl.BlockSpec((c, POOL_WIDTH), lambda g, j: (jnp.minimum(cur(g) % blocks_per_seq, 1), 0)),
            _resident((N_POOL_GROUPS, POOL_GROUP_DIM, POOL_GROUP_DIM)),
            _resident((1, POOL_WIDTH)),
            _resident((1, RET_WIDTH)),
            prev_spec,
            _resident((D_MODEL, D_MODEL)),
            _resident((1, D_MODEL)),
        ],
        out_specs=[prev_spec, prev_spec],
        out_shape=[jax.ShapeDtypeStruct((m, D_MODEL), F32), jax.ShapeDtypeStruct((m, D_MODEL), BF16)],
        scratch_shapes=[
            pltpu.VMEM((2 * c, POOL_WIDTH), BF16),
            pltpu.VMEM((RET_HEADS, RET_HEAD_DIM, RET_HEAD_DIM), F32),
            pltpu.VMEM((2, rows, D_MODEL), BF16),
        ],
        compiler_params=_params(blk),
        name="mixer",
    )(z, decay, xi_t, zeta_t, band, inv_cnt, pool_w_b, pool_scale, ret_gn_w, x2, w_out_b, norm2_w)


FFN_SLABS = FFN_TN // V7X_LANES


def _ffn_kernel(tiles_per_seq, c_ref, wg_ref, wv_ref, cwg_ref, cwv_ref, cbg_ref, cbv_ref, wd_ref,
                o_ref, hg_scr, hv_scr, ug_scr, uv_scr, acc_scr):
    i = pl.program_id(0)
    j = pl.program_id(1)
    hh = V7X_SUBLANES

    @pl.when(jnp.logical_and(i == 0, j == 0))
    def _():
        hg_scr[...] = jnp.zeros_like(hg_scr)
        hv_scr[...] = jnp.zeros_like(hv_scr)
        acc_scr[...] = jnp.zeros_like(acc_scr)

    ug_scr[:, 0:hh, :] = hg_scr[j]
    uv_scr[:, 0:hh, :] = hv_scr[j]

    c_b = c_ref[...]
    ug = jnp.dot(c_b, wg_ref[...], preferred_element_type=F32)
    uv = jnp.dot(c_b, wv_ref[...], preferred_element_type=F32)
    for ct in range(FFN_SLABS):
        lanes = slice(ct * V7X_LANES, (ct + 1) * V7X_LANES)
        ug_scr[ct, hh:hh + FFN_TM, :] = ug[:, lanes]
        uv_scr[ct, hh:hh + FFN_TM, :] = uv[:, lanes]

    def conv(u_scr, cw, cb):
        outs = []
        for ct in range(FFN_SLABS):
            lanes = slice(ct * V7X_LANES, (ct + 1) * V7X_LANES)
            out = cb[:, lanes] + cw[CONV_WIDTH - 1:CONV_WIDTH, lanes] * u_scr[ct, hh:hh + FFN_TM, :]
            for k in range(1, CONV_WIDTH):
                out = out + cw[CONV_WIDTH - 1 - k:CONV_WIDTH - k, lanes] * u_scr[ct, hh - k:hh - k + FFN_TM, :]
            outs.append(out)
        return jnp.concatenate(outs, axis=1)

    gate = conv(ug_scr, cwg_ref[...], cbg_ref[...])
    val = conv(uv_scr, cwv_ref[...], cbv_ref[...])
    act = (gate / (1.0 + jnp.exp(-gate)) * val).astype(BF16)
    total = jnp.where(j == 0, 0.0, acc_scr[...]) + jnp.dot(act, wd_ref[...], preferred_element_type=F32)
    acc_scr[...] = total
    o_ref[...] = total
    keep = (i + 1) % tiles_per_seq != 0
    hg_scr[j] = jnp.where(keep, ug_scr[:, FFN_TM:FFN_TM + hh, :], 0.0)
    hv_scr[j] = jnp.where(keep, uv_scr[:, FFN_TM:FFN_TM + hh, :], 0.0)


def _ffn(c, w_up_gv, conv_w_gv, conv_b_gv, w_down_p, seq):
    m = c.shape[0]
    nj = D_FF_PAD // FFN_TN
    halo = (nj, FFN_SLABS, V7X_SUBLANES, V7X_LANES)
    slab = (FFN_SLABS, V7X_SUBLANES + FFN_TM, V7X_LANES)
    blk = (2 * (FFN_TM * D_MODEL * 2 + FFN_TM * D_MODEL * 4 + 2 * D_MODEL * FFN_TN * 2 + FFN_TN * D_MODEL * 2)
           + 2 * int(np.prod(halo)) * 4 + 2 * int(np.prod(slab)) * 4 + FFN_TM * D_MODEL * 4)
    return pl.pallas_call(
        functools.partial(_ffn_kernel, seq // FFN_TM),
        grid=(m // FFN_TM, nj),
        in_specs=[
            pl.BlockSpec((FFN_TM, D_MODEL), lambda i, j: (i, 0)),
            pl.BlockSpec((D_MODEL, FFN_TN), lambda i, j: (0, j)),
            pl.BlockSpec((D_MODEL, FFN_TN), lambda i, j: (0, j)),
            pl.BlockSpec((CONV_WIDTH, FFN_TN), lambda i, j: (0, j)),
            pl.BlockSpec((CONV_WIDTH, FFN_TN), lambda i, j: (0, j)),
            pl.BlockSpec((1, FFN_TN), lambda i, j: (0, j)),
            pl.BlockSpec((1, FFN_TN), lambda i, j: (0, j)),
            pl.BlockSpec((FFN_TN, D_MODEL), lambda i, j: (j, 0)),
        ],
        out_specs=pl.BlockSpec((FFN_TM, D_MODEL), lambda i, j: (i, 0)),
        out_shape=jax.ShapeDtypeStruct((m, D_MODEL), F32),
        scratch_shapes=[pltpu.VMEM(halo, F32), pltpu.VMEM(halo, F32), pltpu.VMEM(slab, F32), pltpu.VMEM(slab, F32),
                        pltpu.VMEM((FFN_TM, D_MODEL), F32)],
        compiler_params=_params(blk),
        name="ffn",
    )(c, *w_up_gv, *conv_w_gv, *conv_b_gv, w_down_p)


def _ple_kernel(h_ref, f_ref, p_ref, n3_ref, wg_ref, wp_ref, nf_ref, o_ref, e_scr, pb_scr):
    subs = [slice(s * SUB, (s + 1) * SUB) for s in range(PLE_TM // SUB)]
    chunks = [slice(n * PLE_TN, (n + 1) * PLE_TN) for n in range(D_MODEL // PLE_TN)]
    for rows in subs:
        h2 = h_ref[rows, :] + f_ref[rows, :]
        o_ref[rows, :] = h2
        e_scr[rows, :] = _rms_rows(h2, n3_ref[...]).astype(BF16)
        pb_scr[rows, :] = p_ref[rows, :].astype(BF16)
    for rows in subs:
        for cols in chunks:
            gate_logit = jnp.dot(e_scr[rows, :], wg_ref[:, cols], preferred_element_type=F32)
            emb = jnp.dot(pb_scr[rows, :], wp_ref[:, cols], preferred_element_type=F32)
            o_ref[rows, cols] += emb / (1.0 + jnp.exp(-gate_logit))
    for rows in subs:
        o_ref[rows, :] = _rms_rows(o_ref[rows, :], nf_ref[...])


def _ple(h1, f, p2, norm3_w, ple_gate_b, ple_proj_b, final_norm_w):
    m = h1.shape[0]
    ple_dim = p2.shape[1]
    blk = (2 * (3 * PLE_TM * D_MODEL * 4 + PLE_TM * ple_dim * 4) + D_MODEL * D_MODEL * 2 + ple_dim * D_MODEL * 2
           + PLE_TM * D_MODEL * 2 + PLE_TM * ple_dim * 2)
    row_spec = pl.BlockSpec((PLE_TM, D_MODEL), lambda i, j: (i, 0))
    return pl.pallas_call(
        _ple_kernel,
        grid=(m // PLE_TM, 1),
        in_specs=[
            row_spec,
            row_spec,
            pl.BlockSpec((PLE_TM, ple_dim), lambda i, j: (i, 0)),
            _resident((1, D_MODEL)),
            _resident((D_MODEL, D_MODEL)),
            _resident((ple_dim, D_MODEL)),
            _resident((1, D_MODEL)),
        ],
        out_specs=row_spec,
        out_shape=jax.ShapeDtypeStruct((m, D_MODEL), F32),
        scratch_shapes=[pltpu.VMEM((PLE_TM, D_MODEL), BF16), pltpu.VMEM((PLE_TM, ple_dim), BF16)],
        compiler_params=_params(blk),
        name="ple",
    )(h1, f, p2, norm3_w, ple_gate_b, ple_proj_b, final_norm_w)


def _gate_val(a, dtype):
    pad = [(0, 0)] * (a.ndim - 1) + [(0, D_FF_PAD - D_FF)]
    return (jnp.pad(a[..., :D_FF], pad).astype(dtype), jnp.pad(a[..., D_FF:], pad).astype(dtype))


def kernel(x, p, norm1_w, w_in, pool_w, pool_scale, ret_gn_w, w_out, norm2_w, w_up, conv_w, conv_b, w_down,
           norm3_w, ple_gate_w, ple_proj_w, final_norm_w):
    batch, seq, d_model = x.shape
    assert w_in.shape[0] == 1, "one layer per call"
    assert d_model == D_MODEL and seq % (MIX_CHUNKS * RET_CHUNK) == 0 and seq % FFN_TM == 0
    m = batch * seq
    assert m % IN_TM == 0 and seq % IN_TM == 0 and m % PLE_TM == 0
    row = lambda v: v.reshape(1, -1)

    w_in_b = w_in[0].astype(BF16)
    pool_w_b = pool_w[0].astype(BF16)
    conv_w_gv = _gate_val(conv_w[0], F32)
    conv_b_gv = _gate_val(row(conv_b[0]), F32)
    ple_proj_b = ple_proj_w[0].astype(BF16)

    x2 = x.reshape(m, D_MODEL)
    rope, *tables = _mixer_tables(seq)
    z, w_up_g, w_up_v, w_down_p, w_out_b, ple_gate_b = _in_proj(
        x2, row(norm1_w[0]), w_in_b, rope, w_up[0], w_down[0], w_out[0], ple_gate_w[0], seq)
    h1, c = _mixer(z, tables, pool_w_b, row(pool_scale[0]), row(ret_gn_w[0]), x2, w_out_b, row(norm2_w[0]), batch, seq)
    f = _ffn(c, (w_up_g, w_up_v), conv_w_gv, conv_b_gv, w_down_p, seq)
    out = _ple(h1, f, p[0].reshape(m, -1), row(norm3_w[0]), ple_gate_b, ple_proj_b, row(final_norm_w))
    return out.reshape(batch, seq, D_MODEL)
```
